```python
import jax, jax.numpy as jnp
from jax import lax
import numpy as np

D_MODEL = 1024
BATCH = 8
SEQ = 4096
DEPTH = 1

MLA_HEADS = 8
Q_LORA_RANK = 256
KV_LORA_RANK = 256
QK_NOPE_DIM = 64
QK_ROPE_DIM = 32
V_HEAD_DIM = 64
ROPE_THETA = 10000.0
Q_BLOCK = 128
SWA_HEADS = 8
SWA_KV_HEADS = 2
SWA_HEAD_DIM = 64
WINDOW = 128
BAND_BLOCK = 128
MIX_WIDTH = MLA_HEADS * V_HEAD_DIM + SWA_HEADS * SWA_HEAD_DIM
IN_SPLITS = (Q_LORA_RANK, KV_LORA_RANK, QK_ROPE_DIM,
             SWA_HEADS * SWA_HEAD_DIM, SWA_KV_HEADS * SWA_HEAD_DIM, SWA_KV_HEADS * SWA_HEAD_DIM)
IN_WIDTH = sum(IN_SPLITS)
D_FF = 2816
CONV_WIDTH = 3
LN_EPS = 1e-5
RMS_EPS = 1e-6
DEEPNORM_ALPHA = (2.0 * DEPTH) ** 0.25
DEEPNORM_BETA = (8.0 * DEPTH) ** -0.25
NEG_BIG = -1e30

kernel_name = "hybrid_mla_swa_convffn_deepnorm"


def _layer_norm(x, g, b):
    xf = x.astype(jnp.float32)
    mu = jnp.mean(xf, axis=-1, keepdims=True)
    var = jnp.mean(jnp.square(xf - mu), axis=-1, keepdims=True)
    return ((xf - mu) * lax.rsqrt(var + LN_EPS) * g.astype(jnp.float32) + b.astype(jnp.float32)).astype(x.dtype)


def _rms_norm(x, g):
    xf = x.astype(jnp.float32)
    r = lax.rsqrt(jnp.mean(jnp.square(xf), axis=-1, keepdims=True) + RMS_EPS)
    return (xf * r * g.astype(jnp.float32)).astype(x.dtype)


def _rope_cos_sin(positions):
    inv_freq = ROPE_THETA ** (-jnp.arange(0, QK_ROPE_DIM, 2, dtype=jnp.float32) / QK_ROPE_DIM)
    ang = positions.astype(jnp.float32)[..., None] * inv_freq
    return jnp.cos(ang), jnp.sin(ang)


def _rotate(x, cos, sin):
    xf = x.astype(jnp.float32)
    x1, x2 = jnp.split(xf, 2, axis=-1)
    return jnp.concatenate([x1 * cos - x2 * sin, x1 * sin + x2 * cos], axis=-1).astype(x.dtype)


def _alibi_slopes(n_heads):
    return 2.0 ** (-8.0 * (np.arange(n_heads, dtype=np.float32) + 1.0) / n_heads)


def _mla_attention(q, k, v):
    B, S, H, DQ = q.shape
    nq = S // Q_BLOCK
    qb = q.reshape(B, nq, Q_BLOCK, H, DQ).transpose(1, 0, 2, 3, 4)
    scale = DQ ** -0.5

    def one_block(q_blk):
        s = jnp.einsum('bqhd,bkhd->bhqk', q_blk, k).astype(jnp.float32) * scale
        p = jax.nn.softmax(s, axis=-1).astype(v.dtype)
        return jnp.einsum('bhqk,bkhd->bqhd', p, v)

    o = lax.map(one_block, qb)
    return o.transpose(1, 0, 2, 3, 4).reshape(B, S, H * v.shape[-1])


def _band(t):
    B, S = t.shape[0], t.shape[1]
    nb = S // BAND_BLOCK
    pad = [(0, 0), (BAND_BLOCK, BAND_BLOCK)] + [(0, 0)] * (t.ndim - 2)
    tp = jnp.pad(t, pad).reshape((B, nb + 2, BAND_BLOCK) + t.shape[2:])
    return jnp.concatenate([tp[:, :-2], tp[:, 1:-1], tp[:, 2:]], axis=2)


def _window_gqa_attention(q, k, v, positions, sinks):
    B, S, H, D = q.shape
    KVH = k.shape[2]
    G = H // KVH
    nb = S // BAND_BLOCK
    qb = q.reshape(B, nb, BAND_BLOCK, KVH, G, D)
    kb, vb = _band(k), _band(v)
    pk = _band(positions)
    pq = positions.reshape(B, nb, BAND_BLOCK)
    dist = jnp.abs(pq[..., :, None] - pk[..., None, :]).astype(jnp.float32)
    a = jnp.arange(BAND_BLOCK)[:, None]
    c = jnp.arange(3 * BAND_BLOCK)[None, :]
    rel = c - BAND_BLOCK - a
    key_idx = jnp.arange(nb)[:, None, None] * BAND_BLOCK - BAND_BLOCK + c[None]
    mask = (jnp.abs(rel)[None] <= WINDOW) & (key_idx >= 0) & (key_idx < S)

    s = jnp.einsum('bnqkgd,bnckd->bnkgqc', qb, kb).astype(jnp.float32) * (D ** -0.5)
    slopes = jnp.asarray(_alibi_slopes(H)).reshape(KVH, G)[:, :, None, None]
    s = s - slopes * dist[:, :, None, None]
    s = jnp.where(mask[None, :, None, None], s, NEG_BIG)
    sk = sinks.astype(jnp.float32).reshape(KVH, G)[:, :, None, None]
    m = jnp.maximum(jnp.max(s, axis=-1, keepdims=True), sk)
    e = jnp.exp(s - m)
    p = e / (jnp.sum(e, axis=-1, keepdims=True) + jnp.exp(sk - m))
    o = jnp.einsum('bnkgqc,bnckd->bnqkgd', p.astype(v.dtype), vb)
    return o.reshape(B, S, H * D)


def _conv_ffn(x, w_up, conv_w, conv_b, w_down):
    h = jnp.einsum('bsd,df->bsf', x, w_up)
    h = lax.conv_general_dilated(
        h, conv_w, window_strides=(1,), padding=((CONV_WIDTH // 2, CONV_WIDTH // 2),),
        dimension_numbers=('NWC', 'WIO', 'NWC'), feature_group_count=h.shape[-1]) + conv_b
    g, u = jnp.split(h, 2, axis=-1)
    return jnp.einsum('bsf,fd->bsd', jax.nn.gelu(g, approximate=False) * u, w_down)


def setup_inputs(seed: int = 0) -> dict:
    key = jax.random.key(seed)
    ks = jax.random.split(key, 20)
    f32 = jnp.float32

    def nrm(k, shape, scale):
        return jax.random.normal(k, shape, f32) * scale

    x = jax.random.normal(ks[0], (BATCH, SEQ, D_MODEL), f32)
    positions = jnp.broadcast_to(jnp.arange(SEQ, dtype=jnp.int32)[None, :], (BATCH, SEQ))
    return {
        "x": x,
        "positions": positions,
        "w_in": nrm(ks[1], (D_MODEL, IN_WIDTH), D_MODEL ** -0.5),
        "q_norm_g": 1.0 + nrm(ks[2], (Q_LORA_RANK,), 0.02),
        "w_q_b": nrm(ks[3], (Q_LORA_RANK, MLA_HEADS * (QK_NOPE_DIM + QK_ROPE_DIM)), Q_LORA_RANK ** -0.5),
        "kv_norm_g": 1.0 + nrm(ks[4], (KV_LORA_RANK,), 0.02),
        "w_kv_b": nrm(ks[5], (KV_LORA_RANK, MLA_HEADS * (QK_NOPE_DIM + V_HEAD_DIM)), KV_LORA_RANK ** -0.5),
        "swa_sinks": nrm(ks[6], (SWA_HEADS,), 0.5),
        "w_o": nrm(ks[7], (MIX_WIDTH, D_MODEL), MIX_WIDTH ** -0.5) * DEEPNORM_BETA,
        "ln1_g": 1.0 + nrm(ks[8], (D_MODEL,), 0.02),
        "ln1_b": nrm(ks[9], (D_MODEL,), 0.02),
        "w_up": nrm(ks[10], (D_MODEL, 2 * D_FF), D_MODEL ** -0.5),
        "conv_w": nrm(ks[11], (CONV_WIDTH, 1, 2 * D_FF), CONV_WIDTH ** -0.5),
        "conv_b": nrm(ks[12], (2 * D_FF,), 0.02),
        "w_down": nrm(ks[13], (D_FF, D_MODEL), D_FF ** -0.5) * DEEPNORM_BETA,
        "ln2_g": 1.0 + nrm(ks[14], (D_MODEL,), 0.02),
        "ln2_b": nrm(ks[15], (D_MODEL,), 0.02),
    }


def reference(x, positions, w_in, q_norm_g, w_q_b, kv_norm_g, w_kv_b, swa_sinks, w_o,
              ln1_g, ln1_b, w_up, conv_w, conv_b, w_down, ln2_g, ln2_b):
    B, S, _ = x.shape
    cos, sin = _rope_cos_sin(positions)
    for _layer in range(DEPTH):
        proj = jnp.einsum('bsd,df->bsf', x, w_in)
        offs = np.cumsum(IN_SPLITS)[:-1].tolist()
        c_q, c_kv, k_rope, q_s, k_s, v_s = jnp.split(proj, offs, axis=-1)

        q = jnp.einsum('bsr,rf->bsf', _rms_norm(c_q, q_norm_g), w_q_b)
        q = q.reshape(B, S, MLA_HEADS, QK_NOPE_DIM + QK_ROPE_DIM)
        q_nope, q_rope = q[..., :QK_NOPE_DIM], q[..., QK_NOPE_DIM:]
        q_rope = _rotate(q_rope, cos[:, :, None, :], sin[:, :, None, :])
        kv = jnp.einsum('bsr,rf->bsf', _rms_norm(c_kv, kv_norm_g), w_kv_b)
        kv = kv.reshape(B, S, MLA_HEADS, QK_NOPE_DIM + V_HEAD_DIM)
        k_nope, v_mla = kv[..., :QK_NOPE_DIM], kv[..., QK_NOPE_DIM:]
        k_rope = _rotate(k_rope, cos, sin)
        q_mla = jnp.concatenate([q_nope, q_rope], axis=-1)
        k_mla = jnp.concatenate(
            [k_nope, jnp.broadcast_to(k_rope[:, :, None, :], (B, S, MLA_HEADS, QK_ROPE_DIM))], axis=-1)
        o_mla = _mla_attention(q_mla, k_mla, v_mla)

        o_swa = _window_gqa_attention(
            q_s.reshape(B, S, SWA_HEADS, SWA_HEAD_DIM),
            k_s.reshape(B, S, SWA_KV_HEADS, SWA_HEAD_DIM),
            v_s.reshape(B, S, SWA_KV_HEADS, SWA_HEAD_DIM),
            positions, swa_sinks)

        mix = jnp.einsum('bsf,fd->bsd', jnp.concatenate([o_mla, o_swa], axis=-1), w_o)
        x = _layer_norm(DEEPNORM_ALPHA * x + mix, ln1_g, ln1_b)

        ff = _conv_ffn(x, w_up, conv_w, conv_b, w_down)
        x = _layer_norm(DEEPNORM_ALPHA * x + ff, ln2_g, ln2_b)
    return x
```

```python
import functools

import numpy as np
import jax
import jax.numpy as jnp
from jax import lax
from jax.experimental import pallas as pl
from jax.experimental.pallas import tpu as pltpu

F32 = jnp.float32
BF16 = jnp.bfloat16

D_MODEL = 1024
MLA_HEADS = 8
Q_LORA = 256
KV_LORA = 256
NOPE = 64
ROPE = 32
HALF_ROPE = ROPE // 2
V_DIM = 64
ROPE_THETA = 10000.0
SWA_HEADS = 8
SWA_KV_HEADS = 2
SWA_GROUP = SWA_HEADS // SWA_KV_HEADS
SWA_DIM = 64
WINDOW = 128
BAND = 128
D_FF = 2816
LN_EPS = 1e-5
RMS_EPS = 1e-6
DEPTH = 1
ALPHA = (2.0 * DEPTH) ** 0.25
NEG_BIG = -1e30

LANES = 128
SUBLANES = 8
VMEM_LIMIT = 56 * 1024 * 1024

PROJ_TM = 512
MLA_TQ = 256
OPROJ_TM = 512
FFN_TM = 512
FFN_FC = 256
FFN_NCHUNK = D_FF // FFN_FC

_OFF_CQ = 0
_OFF_CKV = _OFF_CQ + Q_LORA
_OFF_QS = _OFF_CKV + KV_LORA
_OFF_KS = _OFF_QS + SWA_HEADS * SWA_DIM
_OFF_VS = _OFF_KS + SWA_KV_HEADS * SWA_DIM
_OFF_KR = _OFF_VS + SWA_KV_HEADS * SWA_DIM
IN_WIDTH_R = _OFF_KR + LANES


def _nt_dot(a, b):
    return lax.dot_general(a, b, (((1,), (1,)), ((), ())), preferred_element_type=F32)


def _rms(c, g):
    r = lax.rsqrt(jnp.mean(c * c, axis=-1, keepdims=True) + RMS_EPS)
    return c * r * g


def _layer_norm(y, g, b):
    mu = jnp.mean(y, axis=-1, keepdims=True)
    d = y - mu
    var = jnp.mean(d * d, axis=-1, keepdims=True)
    return d * lax.rsqrt(var + LN_EPS) * g + b


def _rope_kernel(pos_ref, invf_ref, cos_ref, sin_ref, nsin_ref):
    ang = pos_ref[...].astype(F32) * invf_ref[...]
    cos_ref[...] = jnp.cos(ang)
    s = jnp.sin(ang)
    sin_ref[...] = s
    nsin_ref[...] = -s


def _rope_tables(positions):
    n = positions.size
    inv_freq = ROPE_THETA ** (-jnp.arange(0, ROPE, 2, dtype=F32) / ROPE)
    reps = LANES // HALF_ROPE
    pos_rep = jnp.repeat(positions.reshape(-1), HALF_ROPE).reshape(n // reps, LANES)
    invf = jnp.tile(inv_freq, reps)[None, :]
    rows = n // reps
    shp = jax.ShapeDtypeStruct((rows, LANES), F32)
    cos_c, sin_c, nsin_c = pl.pallas_call(
        _rope_kernel,
        out_shape=(shp, shp, shp),
        name="rope_table",
    )(pos_rep, invf)
    cos16 = cos_c.reshape(n, HALF_ROPE)
    sin16 = sin_c.reshape(n, HALF_ROPE)
    nsin16 = nsin_c.reshape(n, HALF_ROPE)
    ones = jnp.ones((n, NOPE), F32)
    zpad = jnp.zeros((n, LANES - NOPE - ROPE), F32)
    znope = jnp.zeros((n, NOPE), F32)
    c_tab = jnp.concatenate([ones, cos16, cos16, zpad], axis=1)
    s_tab = jnp.concatenate([znope, nsin16, sin16, zpad], axis=1)
    return c_tab, s_tab


def _proj_kernel(x_ref, c_ref, sn_ref, win_ref, gq_ref, wq_ref, gkv_ref, wkv_ref,
                 q_ref, k_ref, v_ref, qs_ref, ks_ref, vs_ref):
    tm = x_ref.shape[0]
    xb = x_ref[...].astype(BF16)
    proj = jnp.dot(xb, win_ref[...], preferred_element_type=F32)
    c_tab = c_ref[...]
    s_tab = sn_ref[...]
    lane = lax.broadcasted_iota(jnp.int32, (tm, LANES), 1)
    lo = lane < SWA_DIM

    cqn = _rms(proj[:, _OFF_CQ:_OFF_CQ + Q_LORA], gq_ref[...]).astype(BF16)
    q = jnp.dot(cqn, wq_ref[...], preferred_element_type=F32)
    q_scale = (NOPE + ROPE) ** -0.5
    for h in range(MLA_HEADS):
        qh = q[:, h * LANES:(h + 1) * LANES]
        qsw = pltpu.roll(qh, LANES - ROPE, 1)
        q_ref[:, h * LANES:(h + 1) * LANES] = ((qh * c_tab + qsw * s_tab) * q_scale).astype(BF16)

    ckvn = _rms(proj[:, _OFF_CKV:_OFF_CKV + KV_LORA], gkv_ref[...]).astype(BF16)
    kv = jnp.dot(ckvn, wkv_ref[...], preferred_element_type=F32)
    kr_blk = proj[:, _OFF_KR:_OFF_KR + LANES]
    kr_sw = pltpu.roll(kr_blk, LANES // 2, 1)
    rope_lanes = (lane >= NOPE) & (lane < NOPE + ROPE)
    kr = jnp.where(rope_lanes, kr_blk * c_tab + kr_sw * s_tab, 0.0)
    for h in range(MLA_HEADS):
        k_ref[:, h * LANES:(h + 1) * LANES] = (kv[:, h * LANES:(h + 1) * LANES] + kr).astype(BF16)
    v_ref[...] = kv[:, MLA_HEADS * LANES:].astype(BF16)

    qs_ref[...] = (proj[:, _OFF_QS:_OFF_QS + SWA_HEADS * SWA_DIM] * (SWA_DIM ** -0.5)).astype(BF16)
    ks = proj[:, _OFF_KS:_OFF_KS + LANES]
    ksr = pltpu.roll(ks, LANES // 2, 1)
    ks_ref[:, 0 * LANES:1 * LANES] = jnp.where(lo, ks, 0.0).astype(BF16)
    ks_ref[:, 1 * LANES:2 * LANES] = jnp.where(lo, 0.0, ksr).astype(BF16)
    ks_ref[:, 2 * LANES:3 * LANES] = jnp.where(lo, ksr, 0.0).astype(BF16)
    ks_ref[:, 3 * LANES:4 * LANES] = jnp.where(lo, 0.0, ks).astype(BF16)
    vs = proj[:, _OFF_VS:_OFF_VS + LANES]
    vsr = pltpu.roll(vs, LANES // 2, 1)
    vs_ref[:, 0 * LANES:1 * LANES] = jnp.where(lo, vs, vsr).astype(BF16)
    vs_ref[:, 1 * LANES:2 * LANES] = jnp.where(lo, vsr, vs).astype(BF16)


def _relayout_weights(w_in, w_q_b, w_kv_b):
    d = w_in.shape[0]
    o = np.cumsum((Q_LORA, KV_LORA, ROPE, SWA_HEADS * SWA_DIM, SWA_KV_HEADS * SWA_DIM,
                   SWA_KV_HEADS * SWA_DIM)).tolist()
    w_cq, w_ckv, w_kr = w_in[:, :o[0]], w_in[:, o[0]:o[1]], w_in[:, o[1]:o[2]]
    w_qs, w_ks, w_vs = w_in[:, o[2]:o[3]], w_in[:, o[3]:o[4]], w_in[:, o[4]:o[5]]
    x1, x2 = w_kr[:, :HALF_ROPE], w_kr[:, HALF_ROPE:]
    z32 = jnp.zeros((d, LANES // 2 - ROPE), w_in.dtype)
    kr_blk = jnp.concatenate([x2, x1, z32, x1, x2, z32], axis=1)
    win_r = jnp.concatenate([w_cq, w_ckv, w_qs, w_ks, w_vs, kr_blk], axis=1).astype(BF16)

    wq = w_q_b.reshape(Q_LORA, MLA_HEADS, NOPE + ROPE)
    qn, q1, q2 = wq[..., :NOPE], wq[..., NOPE:NOPE + HALF_ROPE], wq[..., NOPE + HALF_ROPE:]
    wq_r = jnp.concatenate([qn, q1, q2, q2, q1], axis=-1).reshape(Q_LORA, MLA_HEADS * LANES).astype(BF16)

    wkv = w_kv_b.reshape(KV_LORA, MLA_HEADS, NOPE + V_DIM)
    kn = jnp.concatenate([wkv[..., :NOPE], jnp.zeros((KV_LORA, MLA_HEADS, LANES - NOPE), w_kv_b.dtype)], axis=-1)
    wkv_r = jnp.concatenate([kn.reshape(KV_LORA, MLA_HEADS * LANES),
                             wkv[..., NOPE:].reshape(KV_LORA, MLA_HEADS * V_DIM)], axis=1).astype(BF16)
    return win_r, wq_r, wkv_r


def _projections(x2d, c_tab, s_tab, win_r, gq, wq_r, gkv, wkv_r):
    n = x2d.shape[0]
    tm = PROJ_TM
    row = lambda w: pl.BlockSpec((tm, w), lambda i: (i, 0))
    full = lambda a: pl.BlockSpec(a.shape, lambda i: (0,) * a.ndim)
    widths = (MLA_HEADS * LANES, MLA_HEADS * LANES, MLA_HEADS * V_DIM,
              SWA_HEADS * SWA_DIM, 4 * LANES, 2 * LANES)
    return pl.pallas_call(
        _proj_kernel,
        grid=(n // tm,),
        in_specs=[row(D_MODEL), row(LANES), row(LANES), full(win_r), full(gq), full(wq_r),
                  full(gkv), full(wkv_r)],
        out_specs=[row(w) for w in widths],
        out_shape=[jax.ShapeDtypeStruct((n, w), BF16) for w in widths],
        compiler_params=pltpu.CompilerParams(dimension_semantics=("parallel",),
                                             vmem_limit_bytes=VMEM_LIMIT),
        name="proj",
    )(x2d, c_tab, s_tab, win_r, gq, wq_r, gkv, wkv_r)


def _mla_kernel(q_ref, k_ref, v_ref, o_ref):
    tq = q_ref.shape[1]
    v = v_ref[0]
    outs = []
    for hh in range(2):
        q = q_ref[0, :, hh * LANES:(hh + 1) * LANES]
        k = k_ref[0, :, hh * LANES:(hh + 1) * LANES]
        s = _nt_dot(q, k)
        m = jnp.max(s, axis=-1, keepdims=True)
        p = jnp.exp(s - m)
        l = jnp.sum(p, axis=-1, keepdims=True)
        o = jnp.dot(p.astype(BF16), v, preferred_element_type=F32)
        outs.append(o / l)
    lane = lax.broadcasted_iota(jnp.int32, (tq, LANES), 1)
    o_ref[0] = jnp.where(lane < V_DIM, outs[0], outs[1]).astype(BF16)


def _mla_attention(q, k, v):
    b, s, _ = q.shape
    tq = MLA_TQ
    return pl.pallas_call(
        _mla_kernel,
        grid=(b, MLA_HEADS // 2, s // tq),
        in_specs=[pl.BlockSpec((1, tq, 2 * LANES), lambda bi, hp, i: (bi, i, hp)),
                  pl.BlockSpec((1, s, 2 * LANES), lambda bi, hp, i: (bi, 0, hp)),
                  pl.BlockSpec((1, s, LANES), lambda bi, hp, i: (bi, 0, hp))],
        out_specs=pl.BlockSpec((1, tq, LANES), lambda bi, hp, i: (bi, i, hp)),
        out_shape=jax.ShapeDtypeStruct((b, s, MLA_HEADS * V_DIM), BF16),
        compiler_params=pltpu.CompilerParams(
            dimension_semantics=("parallel", "parallel", "parallel"),
            vmem_limit_bytes=VMEM_LIMIT),
        name="mla_attn",
    )(q, k, v)


def _alibi_slopes(n_heads):
    return 2.0 ** (-8.0 * (np.arange(n_heads, dtype=np.float32) + 1.0) / n_heads)


def _swa_kernel(sink_ref, qs_ref, ks_ref, vs_ref, pcol_ref, prow_ref, o_ref):
    nb = prow_ref.shape[1]
    i = pl.program_id(1)
    start_blk = jnp.clip(i - 1, 0, nb - 3)
    start = pl.multiple_of(start_blk * BAND, BAND)
    win = 3 * BAND
    pq = pcol_ref[0].astype(F32)
    pk = jnp.concatenate([prow_ref[0, pl.ds(start_blk + j, 1), :] for j in range(3)],
                         axis=-1).astype(F32)
    dist = jnp.abs(pq - pk)
    qi = lax.broadcasted_iota(jnp.int32, (BAND, win), 0)
    ci = lax.broadcasted_iota(jnp.int32, (BAND, win), 1)
    rel = ci - qi + (start - i * BAND)
    mask = jnp.abs(rel) <= WINDOW
    lane = lax.broadcasted_iota(jnp.int32, (BAND, LANES), 1)
    lo = lane < SWA_DIM
    slopes = _alibi_slopes(SWA_HEADS)
    for kvh in range(SWA_KV_HEADS):
        lhs = jnp.concatenate([qs_ref[0, :, (2 * kvh) * LANES:(2 * kvh + 1) * LANES],
                               qs_ref[0, :, (2 * kvh + 1) * LANES:(2 * kvh + 2) * LANES]], axis=0)
        k_lo = ks_ref[0, pl.ds(start, win), (2 * kvh) * LANES:(2 * kvh + 1) * LANES]
        k_hi = ks_ref[0, pl.ds(start, win), (2 * kvh + 1) * LANES:(2 * kvh + 2) * LANES]
        s_all = _nt_dot(lhs, jnp.concatenate([k_lo, k_hi], axis=0))
        vv = vs_ref[0, pl.ds(start, win), kvh * LANES:(kvh + 1) * LANES]
        es, dens = [], []
        for g in range(SWA_GROUP):
            h = kvh * SWA_GROUP + g
            pair, par = g // 2, g % 2
            s = s_all[pair * BAND:(pair + 1) * BAND, par * win:(par + 1) * win]
            s = s - float(slopes[h]) * dist
            s = jnp.where(mask, s, NEG_BIG)
            sk = sink_ref[h]
            m = jnp.maximum(jnp.max(s, axis=-1, keepdims=True), sk)
            e = jnp.exp(s - m)
            dens.append(jnp.sum(e, axis=-1, keepdims=True) + jnp.exp(sk - m))
            es.append(e.astype(BF16))
        o = jnp.dot(jnp.concatenate(es, axis=0), vv, preferred_element_type=F32)
        for pair in range(2):
            o_even = o[(2 * pair) * BAND:(2 * pair + 1) * BAND] / dens[2 * pair]
            o_odd = o[(2 * pair + 1) * BAND:(2 * pair + 2) * BAND] / dens[2 * pair + 1]
            col = (kvh * 2 + pair) * LANES
            o_ref[0, :, col:col + LANES] = jnp.where(lo, o_even, o_odd).astype(BF16)


def _swa_attention(qs, ks4, vs2, positions, sinks):
    b, s, _ = qs.shape
    nb = s // BAND
    pcol = positions.reshape(b, s, 1)
    prow = positions.reshape(b, nb, BAND)
    return pl.pallas_call(
        _swa_kernel,
        grid=(b, nb),
        in_specs=[pl.BlockSpec(memory_space=pltpu.SMEM),
                  pl.BlockSpec((1, BAND, SWA_HEADS * SWA_DIM), lambda bi, i: (bi, i, 0)),
                  pl.BlockSpec((1, s, 4 * LANES), lambda bi, i: (bi, 0, 0)),
                  pl.BlockSpec((1, s, 2 * LANES), lambda bi, i: (bi, 0, 0)),
                  pl.BlockSpec((1, BAND, 1), lambda bi, i: (bi, i, 0)),
                  pl.BlockSpec((1, nb, BAND), lambda bi, i: (bi, 0, 0))],
        out_specs=pl.BlockSpec((1, BAND, SWA_HEADS * SWA_DIM), lambda bi, i: (bi, i, 0)),
        out_shape=jax.ShapeDtypeStruct((b, s, SWA_HEADS * SWA_DIM), BF16),
        compiler_params=pltpu.CompilerParams(dimension_semantics=("parallel", "parallel"),
                                             vmem_limit_bytes=VMEM_LIMIT),
        name="swa_attn",
    )(sinks.astype(F32), qs, ks4, vs2, pcol, prow)


def _oproj_kernel(x_ref, om_ref, os_ref, wo1_ref, wo2_ref, g_ref, b_ref, y_ref):
    mix = jnp.dot(om_ref[...], wo1_ref[...], preferred_element_type=F32)
    mix = mix + jnp.dot(os_ref[...], wo2_ref[...], preferred_element_type=F32)
    y_ref[...] = _layer_norm(ALPHA * x_ref[...] + mix, g_ref[...], b_ref[...])


def _oproj_ln(x2d, om, osw, wo1, wo2, g, b):
    n = x2d.shape[0]
    tm = OPROJ_TM
    row = lambda w: pl.BlockSpec((tm, w), lambda i: (i, 0))
    full = lambda a: pl.BlockSpec(a.shape, lambda i: (0,) * a.ndim)
    return pl.pallas_call(
        _oproj_kernel,
        grid=(n // tm,),
        in_specs=[row(D_MODEL), row(om.shape[1]), row(osw.shape[1]), full(wo1), full(wo2),
                  full(g), full(b)],
        out_specs=row(D_MODEL),
        out_shape=jax.ShapeDtypeStruct((n, D_MODEL), F32),
        compiler_params=pltpu.CompilerParams(dimension_semantics=("parallel",),
                                             vmem_limit_bytes=VMEM_LIMIT),
        name="oproj_ln",
    )(x2d, om, osw, wo1, wo2, g, b)


def _ffn_kernel(seq_tiles, x_ref, xp_ref, xn_ref, wup_ref, cw_ref, cb_ref, wdn_ref, g_ref, b_ref,
                y_ref, acc_ref):
    tm = x_ref.shape[0]
    i = pl.program_id(0)
    t = i % seq_tiles
    x = x_ref[...]
    xp = jnp.where(t > 0, xp_ref[...], 0.0)
    xn = jnp.where(t < seq_tiles - 1, xn_ref[...], 0.0)
    xe = jnp.concatenate([xp, x, xn], axis=0).astype(BF16)
    rows = tm + 2 * SUBLANES
    acc_ref[...] = jnp.zeros_like(acc_ref)

    def conv(h, j):
        w = cw_ref[j]
        h_prev = pltpu.roll(h, 1, 0)[SUBLANES:SUBLANES + tm]
        h_next = pltpu.roll(h, rows - 1, 0)[SUBLANES:SUBLANES + tm]
        h_mid = h[SUBLANES:SUBLANES + tm]
        return h_prev * w[0:1] + h_mid * w[1:2] + h_next * w[2:3] + cb_ref[j]

    def body(j, carry):
        hg = jnp.dot(xe, wup_ref[j], preferred_element_type=F32)
        hu = jnp.dot(xe, wup_ref[FFN_NCHUNK + j], preferred_element_type=F32)
        gt = conv(hg, j)
        ut = conv(hu, FFN_NCHUNK + j)
        a = (0.5 * gt * (1.0 + lax.erf(gt * (2.0 ** -0.5))) * ut).astype(BF16)
        acc_ref[...] += jnp.dot(a, wdn_ref[j], preferred_element_type=F32)
        return carry

    lax.fori_loop(0, FFN_NCHUNK, body, 0)
    y_ref[...] = _layer_norm(ALPHA * x + acc_ref[...], g_ref[...], b_ref[...])


def _ffn_ln(x1, seq_len, wup_r, cw_r, cb_r, wdn_r, g, b):
    n = x1.shape[0]
    tm = FFN_TM
    seq_tiles = seq_len // tm
    hb = tm // SUBLANES
    nhb = n // SUBLANES
    full = lambda a: pl.BlockSpec(a.shape, lambda i: (0,) * a.ndim)
    return pl.pallas_call(
        functools.partial(_ffn_kernel, seq_tiles),
        grid=(n // tm,),
        in_specs=[pl.BlockSpec((tm, D_MODEL), lambda i: (i, 0)),
                  pl.BlockSpec((SUBLANES, D_MODEL), lambda i: (jnp.maximum(i * hb - 1, 0), 0)),
                  pl.BlockSpec((SUBLANES, D_MODEL), lambda i: (jnp.minimum((i + 1) * hb, nhb - 1), 0)),
                  full(wup_r), full(cw_r), full(cb_r), full(wdn_r), full(g), full(b)],
        out_specs=pl.BlockSpec((tm, D_MODEL), lambda i: (i, 0)),
        out_shape=jax.ShapeDtypeStruct((n, D_MODEL), F32),
        scratch_shapes=[pltpu.VMEM((tm, D_MODEL), F32)],
        compiler_params=pltpu.CompilerParams(dimension_semantics=("parallel",),
                                             vmem_limit_bytes=VMEM_LIMIT),
        name="ffn_ln",
    )(x1, x1, x1, wup_r, cw_r, cb_r, wdn_r, g, b)


def kernel(x, positions, w_in, q_norm_g, w_q_b, kv_norm_g, w_kv_b, swa_sinks, w_o,
           ln1_g, ln1_b, w_up, conv_w, conv_b, w_down, ln2_g, ln2_b):
    b, s, d = x.shape
    n = b * s
    x2d = x.reshape(n, d)

    c_tab, s_tab = _rope_tables(positions)
    win_r, wq_r, wkv_r = _relayout_weights(w_in, w_q_b, w_kv_b)
    q, k, v, qs, ks4, vs2 = _projections(
        x2d, c_tab, s_tab, win_r, q_norm_g.reshape(1, -1), wq_r, kv_norm_g.reshape(1, -1), wkv_r)

    o_mla = _mla_attention(q.reshape(b, s, -1), k.reshape(b, s, -1), v.reshape(b, s, -1))
    o_swa = _swa_attention(qs.reshape(b, s, -1), ks4.reshape(b, s, -1), vs2.reshape(b, s, -1),
                           positions, swa_sinks)

    n_mla = MLA_HEADS * V_DIM
    wo = w_o.astype(BF16)
    x1 = _oproj_ln(x2d, o_mla.reshape(n, -1), o_swa.reshape(n, -1), wo[:n_mla], wo[n_mla:],
                   ln1_g.reshape(1, -1), ln1_b.reshape(1, -1))

    nch = 2 * FFN_NCHUNK
    wup_r = w_up.astype(BF16).reshape(d, nch, FFN_FC).transpose(1, 0, 2)
    cw_r = conv_w.reshape(3, nch, FFN_FC).transpose(1, 0, 2)
    cb_r = conv_b.reshape(nch, 1, FFN_FC)
    wdn_r = w_down.astype(BF16).reshape(FFN_NCHUNK, FFN_FC, d)
    y = _ffn_ln(x1, s, wup_r, cw_r, cb_r, wdn_r, ln2_g.reshape(1, -1), ln2_b.reshape(1, -1))
    return y.reshape(b, s, d)
```

```python
import functools

import numpy as np
import jax
import jax.numpy as jnp
from jax import lax
from jax.experimental import pallas as pl
from jax.experimental.pallas import tpu as pltpu

F32 = jnp.float32
BF16 = jnp.bfloat16

D_MODEL = 1024
MLA_HEADS = 8
Q_LORA = 256
KV_LORA = 256
NOPE = 64
ROPE = 32
HALF_ROPE = ROPE // 2
V_DIM = 64
ROPE_THETA = 10000.0
SWA_HEADS = 8
SWA_KV_HEADS = 2
SWA_GROUP = SWA_HEADS // SWA_KV_HEADS
SWA_DIM = 64
WINDOW = 128
BAND = 128
D_FF = 2816
LN_EPS = 1e-5
RMS_EPS = 1e-6
DEPTH = 1
ALPHA = (2.0 * DEPTH) ** 0.25
NEG_BIG = -1e30
LOG2_E = 1.4426950408889634

LANES = 128
SUBLANES = 8
VMEM_LIMIT = 56 * 1024 * 1024

PROJ_TM = 512
MLA_TQ = 512
MLA_TK = 512
OPROJ_TM = 512
FFN_TM = 512
FFN_FC = 256
FFN_NCHUNK = D_FF // FFN_FC

_OFF_CQ = 0
_OFF_CKV = _OFF_CQ + Q_LORA
_OFF_QS = _OFF_CKV + KV_LORA
_OFF_KS = _OFF_QS + SWA_HEADS * SWA_DIM
_OFF_VS = _OFF_KS + SWA_KV_HEADS * SWA_DIM
_OFF_KR = _OFF_VS + SWA_KV_HEADS * SWA_DIM
IN_WIDTH_R = _OFF_KR + LANES


def _nt_dot(a, b):
    return lax.dot_general(a, b, (((1,), (1,)), ((), ())), preferred_element_type=F32)


def _rms(c, g):
    r = lax.rsqrt(jnp.mean(c * c, axis=-1, keepdims=True) + RMS_EPS)
    return c * r * g


def _layer_norm(y, g, b):
    mu = jnp.mean(y, axis=-1, keepdims=True)
    d = y - mu
    var = jnp.mean(d * d, axis=-1, keepdims=True)
    return d * lax.rsqrt(var + LN_EPS) * g + b


def _rope_kernel(pos_ref, invf_ref, cos_ref, sin_ref, nsin_ref):
    ang = pos_ref[...].astype(F32) * invf_ref[...]
    cos_ref[...] = jnp.cos(ang)
    s = jnp.sin(ang)
    sin_ref[...] = s
    nsin_ref[...] = -s


def _rope_tables(positions):
    n = positions.size
    inv_freq = ROPE_THETA ** (-jnp.arange(0, ROPE, 2, dtype=F32) / ROPE)
    reps = LANES // HALF_ROPE
    pos_rep = jnp.repeat(positions.reshape(-1), HALF_ROPE).reshape(n // reps, LANES)
    invf = jnp.tile(inv_freq, reps)[None, :]
    rows = n // reps
    shp = jax.ShapeDtypeStruct((rows, LANES), F32)
    cos_c, sin_c, nsin_c = pl.pallas_call(
        _rope_kernel,
        out_shape=(shp, shp, shp),
        name="rope_table",
    )(pos_rep, invf)
    cos16 = cos_c.reshape(n, HALF_ROPE)
    sin16 = sin_c.reshape(n, HALF_ROPE)
    nsin16 = nsin_c.reshape(n, HALF_ROPE)
    ones = jnp.ones((n, NOPE), F32)
    zpad = jnp.zeros((n, LANES - NOPE - ROPE), F32)
    znope = jnp.zeros((n, NOPE), F32)
    c_tab = jnp.concatenate([ones, cos16, cos16, zpad], axis=1)
    s_tab = jnp.concatenate([znope, nsin16, sin16, zpad], axis=1)
    return c_tab, s_tab


def _proj_kernel(x_ref, c_ref, sn_ref, win_ref, gq_ref, wq_ref, gkv_ref, wkv_ref,
                 q_ref, k_ref, v_ref, qs_ref, ks_ref, vs_ref):
    tm = x_ref.shape[0]
    xb = x_ref[...].astype(BF16)
    proj = jnp.dot(xb, win_ref[...], preferred_element_type=F32)
    c_tab = c_ref[...]
    s_tab = sn_ref[...]
    lane = lax.broadcasted_iota(jnp.int32, (tm, LANES), 1)
    lo = lane < SWA_DIM

    cqn = _rms(proj[:, _OFF_CQ:_OFF_CQ + Q_LORA], gq_ref[...]).astype(BF16)
    q = jnp.dot(cqn, wq_ref[...], preferred_element_type=F32)
    q_scale = (NOPE + ROPE) ** -0.5 * LOG2_E
    for h in range(MLA_HEADS):
        qh = q[:, h * LANES:(h + 1) * LANES]
        qsw = pltpu.roll(qh, LANES - ROPE, 1)
        q_ref[:, h * LANES:(h + 1) * LANES] = ((qh * c_tab + qsw * s_tab) * q_scale).astype(BF16)

    ckvn = _rms(proj[:, _OFF_CKV:_OFF_CKV + KV_LORA], gkv_ref[...]).astype(BF16)
    kv = jnp.dot(ckvn, wkv_ref[...], preferred_element_type=F32)
    kr_blk = proj[:, _OFF_KR:_OFF_KR + LANES]
    kr_sw = pltpu.roll(kr_blk, LANES // 2, 1)
    rope_lanes = (lane >= NOPE) & (lane < NOPE + ROPE)
    kr = jnp.where(rope_lanes, kr_blk * c_tab + kr_sw * s_tab, 0.0)
    for h in range(MLA_HEADS):
        k_ref[:, h * LANES:(h + 1) * LANES] = (kv[:, h * LANES:(h + 1) * LANES] + kr).astype(BF16)
    v_ref[...] = kv[:, MLA_HEADS * LANES:].astype(BF16)

    qs_ref[...] = (proj[:, _OFF_QS:_OFF_QS + SWA_HEADS * SWA_DIM] * (SWA_DIM ** -0.5)).astype(BF16)
    ks = proj[:, _OFF_KS:_OFF_KS + LANES]
    ksr = pltpu.roll(ks, LANES // 2, 1)
    ks_ref[:, 0 * LANES:1 * LANES] = jnp.where(lo, ks, 0.0).astype(BF16)
    ks_ref[:, 1 * LANES:2 * LANES] = jnp.where(lo, 0.0, ksr).astype(BF16)
    ks_ref[:, 2 * LANES:3 * LANES] = jnp.where(lo, ksr, 0.0).astype(BF16)
    ks_ref[:, 3 * LANES:4 * LANES] = jnp.where(lo, 0.0, ks).astype(BF16)
    vs = proj[:, _OFF_VS:_OFF_VS + LANES]
    vsr = pltpu.roll(vs, LANES // 2, 1)
    vs_ref[:, 0 * LANES:1 * LANES] = jnp.where(lo, vs, vsr).astype(BF16)
    vs_ref[:, 1 * LANES:2 * LANES] = jnp.where(lo, vsr, vs).astype(BF16)


def _relayout_weights(w_in, w_q_b, w_kv_b):
    d = w_in.shape[0]
    o = np.cumsum((Q_LORA, KV_LORA, ROPE, SWA_HEADS * SWA_DIM, SWA_KV_HEADS * SWA_DIM,
                   SWA_KV_HEADS * SWA_DIM)).tolist()
    w_cq, w_ckv, w_kr = w_in[:, :o[0]], w_in[:, o[0]:o[1]], w_in[:, o[1]:o[2]]
    w_qs, w_ks, w_vs = w_in[:, o[2]:o[3]], w_in[:, o[3]:o[4]], w_in[:, o[4]:o[5]]
    x1, x2 = w_kr[:, :HALF_ROPE], w_kr[:, HALF_ROPE:]
    z32 = jnp.zeros((d, LANES // 2 - ROPE), w_in.dtype)
    kr_blk = jnp.concatenate([x2, x1, z32, x1, x2, z32], axis=1)
    win_r = jnp.concatenate([w_cq, w_ckv, w_qs, w_ks, w_vs, kr_blk], axis=1).astype(BF16)

    wq = w_q_b.reshape(Q_LORA, MLA_HEADS, NOPE + ROPE)
    qn, q1, q2 = wq[..., :NOPE], wq[..., NOPE:NOPE + HALF_ROPE], wq[..., NOPE + HALF_ROPE:]
    wq_r = jnp.concatenate([qn, q1, q2, q2, q1], axis=-1).reshape(Q_LORA, MLA_HEADS * LANES).astype(BF16)

    wkv = w_kv_b.reshape(KV_LORA, MLA_HEADS, NOPE + V_DIM)
    kn = jnp.concatenate([wkv[..., :NOPE], jnp.zeros((KV_LORA, MLA_HEADS, LANES - NOPE), w_kv_b.dtype)], axis=-1)
    wkv_r = jnp.concatenate([kn.reshape(KV_LORA, MLA_HEADS * LANES),
                             wkv[..., NOPE:].reshape(KV_LORA, MLA_HEADS * V_DIM)], axis=1).astype(BF16)
    return win_r, wq_r, wkv_r


def _projections(x2d, c_tab, s_tab, win_r, gq, wq_r, gkv, wkv_r):
    n = x2d.shape[0]
    tm = PROJ_TM
    row = lambda w: pl.BlockSpec((tm, w), lambda i: (i, 0))
    full = lambda a: pl.BlockSpec(a.shape, lambda i: (0,) * a.ndim)
    widths = (MLA_HEADS * LANES, MLA_HEADS * LANES, MLA_HEADS * V_DIM,
              SWA_HEADS * SWA_DIM, 4 * LANES, 2 * LANES)
    return pl.pallas_call(
        _proj_kernel,
        grid=(n // tm,),
        in_specs=[row(D_MODEL), row(LANES), row(LANES), full(win_r), full(gq), full(wq_r),
                  full(gkv), full(wkv_r)],
        out_specs=[row(w) for w in widths],
        out_shape=[jax.ShapeDtypeStruct((n, w), BF16) for w in widths],
        compiler_params=pltpu.CompilerParams(dimension_semantics=("parallel",),
                                             vmem_limit_bytes=VMEM_LIMIT),
        name="proj",
    )(x2d, c_tab, s_tab, win_r, gq, wq_r, gkv, wkv_r)


def _mla_kernel(q_ref, k_ref, v_ref, o_ref, vt_ref, st_ref, acc_ref):
    tq = q_ref.shape[1]
    tk = MLA_TK
    nc = k_ref.shape[1] // tk

    @pl.when(pl.program_id(2) == 0)
    def _():
        for c in range(nc):
            vt_ref[c] = v_ref[0, c * tk:(c + 1) * tk, :].astype(F32).T.astype(BF16)

    outs = []
    for hh in range(2):
        q = q_ref[0, :, hh * LANES:(hh + 1) * LANES]

        def scores(c, slot):
            start = pl.multiple_of(c * tk, tk)
            st_ref[slot] = _nt_dot(k_ref[0, pl.ds(start, tk), hh * LANES:(hh + 1) * LANES], q)

        def consume(c, slot, m, l):
            st = st_ref[slot]
            m_new = jnp.maximum(m, jnp.max(st, axis=0, keepdims=True))
            p = jnp.exp2(st - m_new)
            pv = jnp.dot(vt_ref[c, hh * V_DIM:(hh + 1) * V_DIM, :], p.astype(BF16),
                         preferred_element_type=F32)
            alpha = jnp.exp2(m - m_new)
            acc_ref[...] = alpha * acc_ref[...] + pv
            return m_new, alpha * l + jnp.sum(p, axis=0, keepdims=True)

        def pair(j, carry, last):
            m, l = carry
            c0 = 2 * j
            cur = 2 * (j % 2)
            if not last:
                scores(c0 + 2, 2 - cur)
                scores(c0 + 3, 3 - cur)
            m, l = consume(c0, cur, m, l)
            return consume(c0 + 1, cur + 1, m, l)

        acc_ref[...] = jnp.zeros_like(acc_ref)
        scores(0, 0)
        scores(1, 1)
        carry = (jnp.full((1, tq), -jnp.inf, F32), jnp.zeros((1, tq), F32))
        for j in range(nc // 2 - 1):
            carry = pair(j, carry, False)
        _, l = pair(nc // 2 - 1, carry, True)
        outs.append(acc_ref[...] / l)
    o_ref[0] = jnp.concatenate(outs, axis=0).T.astype(BF16)


def _mla_attention(q, k, v):
    b, s, _ = q.shape
    tq = MLA_TQ
    return pl.pallas_call(
        _mla_kernel,
        grid=(b, MLA_HEADS // 2, s // tq),
        in_specs=[pl.BlockSpec((1, tq, 2 * LANES), lambda bi, hp, i: (bi, i, hp)),
                  pl.BlockSpec((1, s, 2 * LANES), lambda bi, hp, i: (bi, 0, hp)),
                  pl.BlockSpec((1, s, LANES), lambda bi, hp, i: (bi, 0, hp))],
        out_specs=pl.BlockSpec((1, tq, LANES), lambda bi, hp, i: (bi, i, hp)),
        out_shape=jax.ShapeDtypeStruct((b, s, MLA_HEADS * V_DIM), BF16),
        scratch_shapes=[pltpu.VMEM((s // MLA_TK, LANES, MLA_TK), BF16),
                        pltpu.VMEM((4, MLA_TK, tq), F32),
                        pltpu.VMEM((V_DIM, tq), F32)],
        compiler_params=pltpu.CompilerParams(
            dimension_semantics=("parallel", "parallel", "arbitrary"),
            vmem_limit_bytes=VMEM_LIMIT),
        name="mla_attn",
    )(q, k, v)


def _alibi_slopes(n_heads):
    return 2.0 ** (-8.0 * (np.arange(n_heads, dtype=np.float32) + 1.0) / n_heads)


def _swa_kernel(sink_ref, qs_ref, ks_ref, vs_ref, pcol_ref, prow_ref, o_ref):
    nb = prow_ref.shape[1]
    i = pl.program_id(1)
    start_blk = jnp.clip(i - 1, 0, nb - 3)
    start = pl.multiple_of(start_blk * BAND, BAND)
    win = 3 * BAND
    pq = pcol_ref[0].astype(F32)
    pk = jnp.concatenate([prow_ref[0, pl.ds(start_blk + j, 1), :] for j in range(3)],
                         axis=-1).astype(F32)
    dist = jnp.abs(pq - pk)
    qi = lax.broadcasted_iota(jnp.int32, (BAND, win), 0)
    ci = lax.broadcasted_iota(jnp.int32, (BAND, win), 1)
    rel = ci - qi + (start - i * BAND)
    mask = jnp.abs(rel) <= WINDOW
    lane = lax.broadcasted_iota(jnp.int32, (BAND, LANES), 1)
    lo = lane < SWA_DIM
    slopes = _alibi_slopes(SWA_HEADS)
    for kvh in range(SWA_KV_HEADS):
        lhs = jnp.concatenate([qs_ref[0, :, (2 * kvh) * LANES:(2 * kvh + 1) * LANES],
                               qs_ref[0, :, (2 * kvh + 1) * LANES:(2 * kvh + 2) * LANES]], axis=0)
        k_lo = ks_ref[0, pl.ds(start, win), (2 * kvh) * LANES:(2 * kvh + 1) * LANES]
        k_hi = ks_ref[0, pl.ds(start, win), (2 * kvh + 1) * LANES:(2 * kvh + 2) * LANES]
        s_all = _nt_dot(lhs, jnp.concatenate([k_lo, k_hi], axis=0))
        vv = vs_ref[0, pl.ds(start, win), kvh * LANES:(kvh + 1) * LANES]
        es, dens = [], []
        for g in range(SWA_GROUP):
            h = kvh * SWA_GROUP + g
            pair, par = g // 2, g % 2
            s = s_all[pair * BAND:(pair + 1) * BAND, par * win:(par + 1) * win]
            s = s - float(slopes[h]) * dist
            s = jnp.where(mask, s, NEG_BIG)
            sk = sink_ref[h]
            m = jnp.maximum(jnp.max(s, axis=-1, keepdims=True), sk)
            e = jnp.exp(s - m)
            dens.append(jnp.sum(e, axis=-1, keepdims=True) + jnp.exp(sk - m))
            es.append(e.astype(BF16))
        o = jnp.dot(jnp.concatenate(es, axis=0), vv, preferred_element_type=F32)
        for pair in range(2):
            o_even = o[(2 * pair) * BAND:(2 * pair + 1) * BAND] / dens[2 * pair]
            o_odd = o[(2 * pair + 1) * BAND:(2 * pair + 2) * BAND] / dens[2 * pair + 1]
            col = (kvh * 2 + pair) * LANES
            o_ref[0, :, col:col + LANES] = jnp.where(lo, o_even, o_odd).astype(BF16)


def _swa_attention(qs, ks4, vs2, positions, sinks):
    b, s, _ = qs.shape
    nb = s // BAND
    pcol = positions.reshape(b, s, 1)
    prow = positions.reshape(b, nb, BAND)
    return pl.pallas_call(
        _swa_kernel,
        grid=(b, nb),
        in_specs=[pl.BlockSpec(memory_space=pltpu.SMEM),
                  pl.BlockSpec((1, BAND, SWA_HEADS * SWA_DIM), lambda bi, i: (bi, i, 0)),
                  pl.BlockSpec((1, s, 4 * LANES), lambda bi, i: (bi, 0, 0)),
                  pl.BlockSpec((1, s, 2 * LANES), lambda bi, i: (bi, 0, 0)),
                  pl.BlockSpec((1, BAND, 1), lambda bi, i: (bi, i, 0)),
                  pl.BlockSpec((1, nb, BAND), lambda bi, i: (bi, 0, 0))],
        out_specs=pl.BlockSpec((1, BAND, SWA_HEADS * SWA_DIM), lambda bi, i: (bi, i, 0)),
        out_shape=jax.ShapeDtypeStruct((b, s, SWA_HEADS * SWA_DIM), BF16),
        compiler_params=pltpu.CompilerParams(dimension_semantics=("parallel", "parallel"),
                                             vmem_limit_bytes=VMEM_LIMIT),
        name="swa_attn",
    )(sinks.astype(F32), qs, ks4, vs2, pcol, prow)


def _oproj_kernel(x_ref, om_ref, os_ref, wo1_ref, wo2_ref, g_ref, b_ref, y_ref):
    mix = jnp.dot(om_ref[...], wo1_ref[...], preferred_element_type=F32)
    mix = mix + jnp.dot(os_ref[...], wo2_ref[...], preferred_element_type=F32)
    y_ref[...] = _layer_norm(ALPHA * x_ref[...] + mix, g_ref[...], b_ref[...])


def _oproj_ln(x2d, om, osw, wo1, wo2, g, b):
    n = x2d.shape[0]
    tm = OPROJ_TM
    row = lambda w: pl.BlockSpec((tm, w), lambda i: (i, 0))
    full = lambda a: pl.BlockSpec(a.shape, lambda i: (0,) * a.ndim)
    return pl.pallas_call(
        _oproj_kernel,
        grid=(n // tm,),
        in_specs=[row(D_MODEL), row(om.shape[1]), row(osw.shape[1]), full(wo1), full(wo2),
                  full(g), full(b)],
        out_specs=row(D_MODEL),
        out_shape=jax.ShapeDtypeStruct((n, D_MODEL), F32),
        compiler_params=pltpu.CompilerParams(dimension_semantics=("parallel",),
                                             vmem_limit_bytes=VMEM_LIMIT),
        name="oproj_ln",
    )(x2d, om, osw, wo1, wo2, g, b)


def _ffn_kernel(seq_tiles, x_ref, xp_ref, xn_ref, wup_ref, cw_ref, cb_ref, wdn_ref, g_ref, b_ref,
                y_ref, h_ref, a_ref):
    tm = x_ref.shape[0]
    fc = FFN_FC
    i = pl.program_id(0)
    t = i % seq_tiles
    x = x_ref[...]
    xp = jnp.where(t > 0, xp_ref[...], 0.0)
    xn = jnp.where(t < seq_tiles - 1, xn_ref[...], 0.0)
    xe = jnp.concatenate([xp, x, xn], axis=0).astype(BF16)

    def conv(slot, half, col):
        lanes = slice(half * fc, (half + 1) * fc)
        w = cw_ref[:, col:col + fc]
        h_prev = h_ref[slot, pl.ds(SUBLANES - 1, tm), lanes]
        h_mid = h_ref[slot, pl.ds(SUBLANES, tm), lanes]
        h_next = h_ref[slot, pl.ds(SUBLANES + 1, tm), lanes]
        return h_prev * w[0:1] + h_mid * w[1:2] + h_next * w[2:3] + cb_ref[:, col:col + fc]

    for j in range(FFN_NCHUNK):
        slot = j % 2
        cg, cu = j * fc, D_FF + j * fc
        h_ref[slot, :, 0:fc] = jnp.dot(xe, wup_ref[:, cg:cg + fc], preferred_element_type=F32)
        h_ref[slot, :, fc:2 * fc] = jnp.dot(xe, wup_ref[:, cu:cu + fc], preferred_element_type=F32)
        gt = conv(slot, 0, cg)
        ut = conv(slot, 1, cu)
        a_ref[:, cg:cg + fc] = (0.5 * gt * (1.0 + lax.erf(gt * (2.0 ** -0.5))) * ut).astype(BF16)

    ff = jnp.dot(a_ref[...], wdn_ref[...], preferred_element_type=F32)
    y_ref[...] = _layer_norm(ALPHA * x + ff, g_ref[...], b_ref[...])


def _ffn_ln(x1, seq_len, wup, cw, cb, wdn, g, b):
    n = x1.shape[0]
    tm = FFN_TM
    seq_tiles = seq_len // tm
    hb = tm // SUBLANES
    nhb = n // SUBLANES
    full = lambda a: pl.BlockSpec(a.shape, lambda i: (0,) * a.ndim)
    return pl.pallas_call(
        functools.partial(_ffn_kernel, seq_tiles),
        grid=(n // tm,),
        in_specs=[pl.BlockSpec((tm, D_MODEL), lambda i: (i, 0)),
                  pl.BlockSpec((SUBLANES, D_MODEL), lambda i: (jnp.maximum(i * hb - 1, 0), 0)),
                  pl.BlockSpec((SUBLANES, D_MODEL), lambda i: (jnp.minimum((i + 1) * hb, nhb - 1), 0)),
                  full(wup), full(cw), full(cb), full(wdn), full(g), full(b)],
        out_specs=pl.BlockSpec((tm, D_MODEL), lambda i: (i, 0)),
        out_shape=jax.ShapeDtypeStruct((n, D_MODEL), F32),
        scratch_shapes=[pltpu.VMEM((2, tm + 2 * SUBLANES, 2 * FFN_FC), F32),
                        pltpu.VMEM((tm, D_FF), BF16)],
        compiler_params=pltpu.CompilerParams(dimension_semantics=("parallel",),
                                             vmem_limit_bytes=VMEM_LIMIT),
        name="ffn_ln",
    )(x1, x1, x1, wup, cw, cb, wdn, g, b)


def kernel(x, positions, w_in, q_norm_g, w_q_b, kv_norm_g, w_kv_b, swa_sinks, w_o,
           ln1_g, ln1_b, w_up, conv_w, conv_b, w_down, ln2_g, ln2_b):
    b, s, d = x.shape
    n = b * s
    x2d = x.reshape(n, d)

    c_tab, s_tab = _rope_tables(positions)
    win_r, wq_r, wkv_r = _relayout_weights(w_in, w_q_b, w_kv_b)
    q, k, v, qs, ks4, vs2 = _projections(
        x2d, c_tab, s_tab, win_r, q_norm_g.reshape(1, -1), wq_r, kv_norm_g.reshape(1, -1), wkv_r)

    o_mla = _mla_attention(q.reshape(b, s, -1), k.reshape(b, s, -1), v.reshape(b, s, -1))
    o_swa = _swa_attention(qs.reshape(b, s, -1), ks4.reshape(b, s, -1), vs2.reshape(b, s, -1),
                           positions, swa_sinks)

    n_mla = MLA_HEADS * V_DIM
    wo = w_o.astype(BF16)
    x1 = _oproj_ln(x2d, o_mla.reshape(n, -1), o_swa.reshape(n, -1), wo[:n_mla], wo[n_mla:],
                   ln1_g.reshape(1, -1), ln1_b.reshape(1, -1))

    y = _ffn_ln(x1, s, w_up.astype(BF16), conv_w.reshape(3, 2 * D_FF), conv_b.reshape(1, 2 * D_FF),
                w_down.astype(BF16), ln2_g.reshape(1, -1), ln2_b.reshape(1, -1))
    return y.reshape(b, s, d)
```

```python
import functools

import numpy as np
import jax
import jax.numpy as jnp
from jax import lax
from jax.experimental import pallas as pl
from jax.experimental.pallas import tpu as pltpu

F32 = jnp.float32
BF16 = jnp.bfloat16

D_MODEL = 1024
MLA_HEADS = 8
Q_LORA = 256
KV_LORA = 256
NOPE = 64
ROPE = 32
HALF_ROPE = ROPE // 2
V_DIM = 64
ROPE_THETA = 10000.0
SWA_HEADS = 8
SWA_KV_HEADS = 2
SWA_GROUP = SWA_HEADS // SWA_KV_HEADS
SWA_DIM = 64
WINDOW = 128
BAND = 128
D_FF = 2816
LN_EPS = 1e-5
RMS_EPS = 1e-6
DEPTH = 1
ALPHA = (2.0 * DEPTH) ** 0.25
LOG2_E = 1.4426950408889634
MASK_DIST = 1e30

LANES = 128
SUBLANES = 8
VMEM_LIMIT = 56 * 1024 * 1024

PROJ_TM = 512
SWA_TQ = 512
MLA_TQ = 512
MLA_TK = 512
MLA_VT_ROWS = V_DIM + 16
OPROJ_TM = 512
FFN_TM = 1024
FFN_FC = 256
FFN_NCHUNK = D_FF // FFN_FC

_OFF_CQ = 0
_OFF_CKV = _OFF_CQ + Q_LORA
_OFF_QS = _OFF_CKV + KV_LORA
_OFF_KS = _OFF_QS + SWA_HEADS * SWA_DIM
_OFF_VS = _OFF_KS + SWA_KV_HEADS * SWA_DIM
_OFF_KR = _OFF_VS + SWA_KV_HEADS * SWA_DIM
IN_WIDTH_R = _OFF_KR + LANES


def _nt_dot(a, b):
    return lax.dot_general(a, b, (((1,), (1,)), ((), ())), preferred_element_type=F32)


def _rms(c, g):
    r = lax.rsqrt(jnp.mean(c * c, axis=-1, keepdims=True) + RMS_EPS)
    return c * r * g


def _layer_norm(y, g, b):
    mu = jnp.mean(y, axis=-1, keepdims=True)
    d = y - mu
    var = jnp.mean(d * d, axis=-1, keepdims=True)
    return d * lax.rsqrt(var + LN_EPS) * g + b


def _rope_tables(pos_row, invf_col):
    tm = pos_row.shape[1]
    ang = invf_col * pos_row
    cos_t = jnp.cos(ang)
    sin_t = jnp.sin(ang)
    pad = jnp.zeros((LANES - NOPE - ROPE, tm), F32)
    c_tab = jnp.concatenate([jnp.ones((NOPE, tm), F32), cos_t, cos_t, pad], axis=0).T
    s_tab = jnp.concatenate([jnp.zeros((NOPE, tm), F32), -sin_t, sin_t, pad], axis=0).T
    return c_tab, s_tab


def _proj_kernel(x_ref, pos_ref, invf_ref, win_ref, gq_ref, wq_ref, gkv_ref, wkv_ref,
                 q_ref, k_ref, v_ref, qs_ref, ks_ref, vs_ref):
    tm = x_ref.shape[0]
    xb = x_ref[...].astype(BF16)
    proj = jnp.dot(xb, win_ref[...], preferred_element_type=F32)
    c_tab, s_tab = _rope_tables(pos_ref[0].astype(F32), invf_ref[...])
    lane = lax.broadcasted_iota(jnp.int32, (tm, LANES), 1)
    lo = lane < SWA_DIM

    cqn = _rms(proj[:, _OFF_CQ:_OFF_CQ + Q_LORA], gq_ref[...]).astype(BF16)
    q = jnp.dot(cqn, wq_ref[...], preferred_element_type=F32)
    q_scale = (NOPE + ROPE) ** -0.5 * LOG2_E
    for h in range(MLA_HEADS):
        qh = q[:, h * LANES:(h + 1) * LANES]
        qsw = pltpu.roll(qh, LANES - ROPE, 1)
        q_ref[:, h * LANES:(h + 1) * LANES] = ((qh * c_tab + qsw * s_tab) * q_scale).astype(BF16)

    ckvn = _rms(proj[:, _OFF_CKV:_OFF_CKV + KV_LORA], gkv_ref[...]).astype(BF16)
    kv = jnp.dot(ckvn, wkv_ref[...], preferred_element_type=F32)
    kr_blk = proj[:, _OFF_KR:_OFF_KR + LANES]
    kr_sw = pltpu.roll(kr_blk, LANES // 2, 1)
    rope_lanes = (lane >= NOPE) & (lane < NOPE + ROPE)
    kr = jnp.where(rope_lanes, kr_blk * c_tab + kr_sw * s_tab, 0.0)
    for h in range(MLA_HEADS):
        k_ref[:, h * LANES:(h + 1) * LANES] = (kv[:, h * LANES:(h + 1) * LANES] + kr).astype(BF16)
    v_ref[...] = kv[:, MLA_HEADS * LANES:].astype(BF16)

    qs_scale = SWA_DIM ** -0.5 * LOG2_E
    qs_ref[...] = (proj[:, _OFF_QS:_OFF_QS + SWA_HEADS * SWA_DIM] * qs_scale).astype(BF16)
    ks = proj[:, _OFF_KS:_OFF_KS + LANES]
    ksr = pltpu.roll(ks, LANES // 2, 1)
    ks_ref[:, 0 * LANES:1 * LANES] = jnp.where(lo, ks, 0.0).astype(BF16)
    ks_ref[:, 1 * LANES:2 * LANES] = jnp.where(lo, 0.0, ksr).astype(BF16)
    ks_ref[:, 2 * LANES:3 * LANES] = jnp.where(lo, ksr, 0.0).astype(BF16)
    ks_ref[:, 3 * LANES:4 * LANES] = jnp.where(lo, 0.0, ks).astype(BF16)
    vs = proj[:, _OFF_VS:_OFF_VS + LANES]
    vsr = pltpu.roll(vs, LANES // 2, 1)
    vs_ref[:, 0 * LANES:1 * LANES] = jnp.where(lo, vs, 1.0).astype(BF16)
    vs_ref[:, 1 * LANES:2 * LANES] = jnp.where(lo, vsr, 1.0).astype(BF16)


def _relayout_weights(w_in, w_q_b, w_kv_b):
    d = w_in.shape[0]
    o = np.cumsum((Q_LORA, KV_LORA, ROPE, SWA_HEADS * SWA_DIM, SWA_KV_HEADS * SWA_DIM,
                   SWA_KV_HEADS * SWA_DIM)).tolist()
    w_cq, w_ckv, w_kr = w_in[:, :o[0]], w_in[:, o[0]:o[1]], w_in[:, o[1]:o[2]]
    w_qs, w_ks, w_vs = w_in[:, o[2]:o[3]], w_in[:, o[3]:o[4]], w_in[:, o[4]:o[5]]
    x1, x2 = w_kr[:, :HALF_ROPE], w_kr[:, HALF_ROPE:]
    z32 = jnp.zeros((d, LANES // 2 - ROPE), w_in.dtype)
    kr_blk = jnp.concatenate([x2, x1, z32, x1, x2, z32], axis=1)
    win_r = jnp.concatenate([w_cq, w_ckv, w_qs, w_ks, w_vs, kr_blk], axis=1).astype(BF16)

    wq = w_q_b.reshape(Q_LORA, MLA_HEADS, NOPE + ROPE)
    qn, q1, q2 = wq[..., :NOPE], wq[..., NOPE:NOPE + HALF_ROPE], wq[..., NOPE + HALF_ROPE:]
    wq_r = jnp.concatenate([qn, q1, q2, q2, q1], axis=-1).reshape(Q_LORA, MLA_HEADS * LANES).astype(BF16)

    wkv = w_kv_b.reshape(KV_LORA, MLA_HEADS, NOPE + V_DIM)
    kn = jnp.concatenate([wkv[..., :NOPE], jnp.zeros((KV_LORA, MLA_HEADS, LANES - NOPE), w_kv_b.dtype)], axis=-1)
    wkv_r = jnp.concatenate([kn.reshape(KV_LORA, MLA_HEADS * LANES),
                             wkv[..., NOPE:].reshape(KV_LORA, MLA_HEADS * V_DIM)], axis=1).astype(BF16)
    return win_r, wq_r, wkv_r


def _projections(x2d, positions, win_r, gq, wq_r, gkv, wkv_r):
    n = x2d.shape[0]
    tm = PROJ_TM
    inv_freq = ROPE_THETA ** (-jnp.arange(0, ROPE, 2, dtype=F32) / ROPE)
    invf_col = inv_freq.reshape(HALF_ROPE, 1)
    pos_rows = positions.reshape(n // tm, 1, tm)
    row = lambda w: pl.BlockSpec((tm, w), lambda i: (i, 0))
    full = lambda a: pl.BlockSpec(a.shape, lambda i: (0,) * a.ndim)
    widths = (MLA_HEADS * LANES, MLA_HEADS * LANES, MLA_HEADS * V_DIM,
              SWA_HEADS * SWA_DIM, 4 * LANES, 2 * LANES)
    return pl.pallas_call(
        _proj_kernel,
        grid=(n // tm,),
        in_specs=[row(D_MODEL), pl.BlockSpec((1, 1, tm), lambda i: (i, 0, 0)), full(invf_col),
                  full(win_r), full(gq), full(wq_r), full(gkv), full(wkv_r)],
        out_specs=[row(w) for w in widths],
        out_shape=[jax.ShapeDtypeStruct((n, w), BF16) for w in widths],
        compiler_params=pltpu.CompilerParams(dimension_semantics=("parallel",),
                                             vmem_limit_bytes=VMEM_LIMIT),
        name="proj",
    )(x2d, pos_rows, invf_col, win_r, gq, wq_r, gkv, wkv_r)


def _mla_kernel(q_ref, k_ref, v_ref, o_ref, vt_ref, st_ref, acc_ref):
    tq = q_ref.shape[1]
    tk = MLA_TK
    nc = k_ref.shape[1] // tk

    @pl.when(pl.program_id(2) == 0)
    def _():
        ones = jnp.ones((MLA_VT_ROWS - V_DIM, tk), BF16)
        for c in range(nc):
            vt = v_ref[0, c * tk:(c + 1) * tk, :].astype(F32).T.astype(BF16)
            for hh in range(2):
                vt_ref[c, hh, 0:V_DIM, :] = vt[hh * V_DIM:(hh + 1) * V_DIM]
                vt_ref[c, hh, V_DIM:MLA_VT_ROWS, :] = ones

    outs = []
    for hh in range(2):
        q = q_ref[0, :, hh * LANES:(hh + 1) * LANES]

        def scores(c, slot):
            st_ref[slot] = _nt_dot(k_ref[0, c * tk:(c + 1) * tk, hh * LANES:(hh + 1) * LANES], q)

        def consume(c, slot, m):
            st = st_ref[slot]
            m_new = jnp.maximum(m, jnp.max(st, axis=0, keepdims=True))
            p = jnp.exp2(st - m_new).astype(BF16)
            pv = jnp.dot(vt_ref[c, hh], p, preferred_element_type=F32)
            acc_ref[...] = jnp.exp2(m - m_new) * acc_ref[...] + pv
            return m_new

        acc_ref[...] = jnp.zeros_like(acc_ref)
        scores(0, 0)
        scores(1, 1)
        m = jnp.full((1, tq), -jnp.inf, F32)
        for j in range(nc // 2):
            cur = 2 * (j % 2)
            if j + 1 < nc // 2:
                scores(2 * j + 2, 2 - cur)
                scores(2 * j + 3, 3 - cur)
            m = consume(2 * j, cur, m)
            m = consume(2 * j + 1, cur + 1, m)
        outs.append(acc_ref[0:V_DIM, :] / acc_ref[V_DIM:V_DIM + 1, :])
    o_ref[0] = jnp.concatenate(outs, axis=0).T.astype(BF16)


def _mla_attention(q, k, v):
    b, s, _ = q.shape
    tq = MLA_TQ
    return pl.pallas_call(
        _mla_kernel,
        grid=(b, MLA_HEADS // 2, s // tq),
        in_specs=[pl.BlockSpec((1, tq, 2 * LANES), lambda bi, hp, i: (bi, i, hp)),
                  pl.BlockSpec((1, s, 2 * LANES), lambda bi, hp, i: (bi, 0, hp)),
                  pl.BlockSpec((1, s, LANES), lambda bi, hp, i: (bi, 0, hp))],
        out_specs=pl.BlockSpec((1, tq, LANES), lambda bi, hp, i: (bi, i, hp)),
        out_shape=jax.ShapeDtypeStruct((b, s, MLA_HEADS * V_DIM), BF16),
        scratch_shapes=[pltpu.VMEM((s // MLA_TK, 2, MLA_VT_ROWS, MLA_TK), BF16),
                        pltpu.VMEM((4, MLA_TK, tq), F32),
                        pltpu.VMEM((MLA_VT_ROWS, tq), F32)],
        compiler_params=pltpu.CompilerParams(
            dimension_semantics=("parallel", "parallel", "arbitrary"),
            vmem_limit_bytes=VMEM_LIMIT),
        name="mla_attn",
    )(q, k, v)


def _alibi_slopes(n_heads):
    return 2.0 ** (-8.0 * (np.arange(n_heads, dtype=np.float32) + 1.0) / n_heads)


def _swa_kernel(sink_ref, qs_ref, ks_ref, vs_ref, pcol_ref, prow_ref, o_ref):
    nb = prow_ref.shape[1]
    sub = qs_ref.shape[1] // BAND
    win = 3 * BAND
    qi = lax.broadcasted_iota(jnp.int32, (BAND, win), 0)
    ci = lax.broadcasted_iota(jnp.int32, (BAND, win), 1)
    lane = lax.broadcasted_iota(jnp.int32, (BAND, LANES), 1)
    lo = lane < SWA_DIM
    slopes = _alibi_slopes(SWA_HEADS)

    for jb in range(sub):
        i = pl.program_id(1) * sub + jb
        r0 = jb * BAND
        start_blk = jnp.clip(i - 1, 0, nb - 3)
        start = pl.multiple_of(start_blk * BAND, BAND)
        pq = pcol_ref[0, pl.ds(r0, BAND), :].astype(F32)
        pk = jnp.concatenate([prow_ref[0, pl.ds(start_blk + j, 1), :] for j in range(3)],
                             axis=-1).astype(F32)
        rel = ci - qi + (start - i * BAND)
        dist = jnp.where(jnp.abs(rel) <= WINDOW, jnp.abs(pq - pk), MASK_DIST)
        for kvh in range(SWA_KV_HEADS):
            lhs = jnp.concatenate(
                [qs_ref[0, pl.ds(r0, BAND), (2 * kvh) * LANES:(2 * kvh + 1) * LANES],
                 qs_ref[0, pl.ds(r0, BAND), (2 * kvh + 1) * LANES:(2 * kvh + 2) * LANES]], axis=0)
            k_lo = ks_ref[0, pl.ds(start, win), (2 * kvh) * LANES:(2 * kvh + 1) * LANES]
            k_hi = ks_ref[0, pl.ds(start, win), (2 * kvh + 1) * LANES:(2 * kvh + 2) * LANES]
            s_all = _nt_dot(lhs, jnp.concatenate([k_lo, k_hi], axis=0))
            vv = vs_ref[0, pl.ds(start, win), kvh * LANES:(kvh + 1) * LANES]
            es, sinks = [], []
            for g in range(SWA_GROUP):
                h = kvh * SWA_GROUP + g
                pair, par = g // 2, g % 2
                s = s_all[pair * BAND:(pair + 1) * BAND, par * win:(par + 1) * win]
                s = s - float(slopes[h] * LOG2_E) * dist
                sk = sink_ref[h] * LOG2_E
                m = jnp.maximum(jnp.max(s, axis=-1, keepdims=True), sk)
                sinks.append(jnp.exp2(sk - m))
                es.append(jnp.exp2(s - m).astype(BF16))
            o = jnp.dot(jnp.concatenate(es, axis=0), vv, preferred_element_type=F32)
            res = []
            for g in range(SWA_GROUP):
                og = o[g * BAND:(g + 1) * BAND]
                den = jnp.where(lo, pltpu.roll(og, LANES // 2, 1), og) + sinks[g]
                res.append(og / den)
            for pair in range(2):
                col = (kvh * 2 + pair) * LANES
                o_ref[0, pl.ds(r0, BAND), col:col + LANES] = jnp.where(
                    lo, res[2 * pair], pltpu.roll(res[2 * pair + 1], LANES // 2, 1)).astype(BF16)


def _swa_attention(qs, ks4, vs2, positions, sinks):
    b, s, _ = qs.shape
    nb = s // BAND
    tq = SWA_TQ
    pcol = positions.reshape(b, s, 1)
    prow = positions.reshape(b, nb, BAND)
    return pl.pallas_call(
        _swa_kernel,
        grid=(b, s // tq),
        in_specs=[pl.BlockSpec(memory_space=pltpu.SMEM),
                  pl.BlockSpec((1, tq, SWA_HEADS * SWA_DIM), lambda bi, i: (bi, i, 0)),
                  pl.BlockSpec((1, s, 4 * LANES), lambda bi, i: (bi, 0, 0)),
                  pl.BlockSpec((1, s, 2 * LANES), lambda bi, i: (bi, 0, 0)),
                  pl.BlockSpec((1, tq, 1), lambda bi, i: (bi, i, 0)),
                  pl.BlockSpec((1, nb, BAND), lambda bi, i: (bi, 0, 0))],
        out_specs=pl.BlockSpec((1, tq, SWA_HEADS * SWA_DIM), lambda bi, i: (bi, i, 0)),
        out_shape=jax.ShapeDtypeStruct((b, s, SWA_HEADS * SWA_DIM), BF16),
        compiler_params=pltpu.CompilerParams(dimension_semantics=("parallel", "parallel"),
                                             vmem_limit_bytes=VMEM_LIMIT),
        name="swa_attn",
    )(sinks.astype(F32), qs, ks4, vs2, pcol, prow)


def _oproj_kernel(x_ref, om_ref, os_ref, wo1_ref, wo2_ref, g_ref, b_ref, y_ref):
    mix = jnp.dot(om_ref[...], wo1_ref[...], preferred_element_type=F32)
    mix = mix + jnp.dot(os_ref[...], wo2_ref[...], preferred_element_type=F32)
    y_ref[...] = _layer_norm(ALPHA * x_ref[...] + mix, g_ref[...], b_ref[...])


def _oproj_ln(x2d, om, osw, wo1, wo2, g, b):
    n = x2d.shape[0]
    tm = OPROJ_TM
    row = lambda w: pl.BlockSpec((tm, w), lambda i: (i, 0))
    full = lambda a: pl.BlockSpec(a.shape, lambda i: (0,) * a.ndim)
    return pl.pallas_call(
        _oproj_kernel,
        grid=(n // tm,),
        in_specs=[row(D_MODEL), row(om.shape[1]), row(osw.shape[1]), full(wo1), full(wo2),
                  full(g), full(b)],
        out_specs=row(D_MODEL),
        out_shape=jax.ShapeDtypeStruct((n, D_MODEL), F32),
        compiler_params=pltpu.CompilerParams(dimension_semantics=("parallel",),
                                             vmem_limit_bytes=VMEM_LIMIT),
        name="oproj_ln",
    )(x2d, om, osw, wo1, wo2, g, b)


def _ffn_kernel(seq_tiles, x_ref, xp_ref, xn_ref, wup_ref, cw_ref, cb_ref, wdn_ref, g_ref, b_ref,
                y_ref, h_ref, ai_ref, a_ref):
    tm = x_ref.shape[0]
    fc = FFN_FC
    half = tm // 2
    nslab = fc // LANES
    i = pl.program_id(0)
    t = i % seq_tiles
    x = x_ref[...]
    xp = jnp.where(t > 0, xp_ref[...], 0.0)
    xn = jnp.where(t < seq_tiles - 1, xn_ref[...], 0.0)
    xe = jnp.concatenate([xp, x, xn], axis=0).astype(BF16)

    def conv_even_odd(slot, slab, col):
        e = [h_ref[slot, slab, pl.ds(SUBLANES - 1 + k, half, stride=2), :] for k in range(4)]
        w = cw_ref[:, col:col + LANES]
        bias = cb_ref[:, col:col + LANES]
        even = e[0] * w[0:1] + e[1] * w[1:2] + e[2] * w[2:3] + bias
        odd = e[1] * w[0:1] + e[2] * w[1:2] + e[3] * w[2:3] + bias
        return even, odd

    def gate(gt, ut):
        return 0.5 * gt * (1.0 + lax.erf(gt * (2.0 ** -0.5))) * ut

    for j in range(FFN_NCHUNK):
        slot = j % 2
        cg, cu = j * fc, D_FF + j * fc
        hg = jnp.dot(xe, wup_ref[:, cg:cg + fc], preferred_element_type=F32)
        hu = jnp.dot(xe, wup_ref[:, cu:cu + fc], preferred_element_type=F32)
        for sl in range(nslab):
            h_ref[slot, sl] = hg[:, sl * LANES:(sl + 1) * LANES]
            h_ref[slot, nslab + sl] = hu[:, sl * LANES:(sl + 1) * LANES]
        for sl in range(nslab):
            g_even, g_odd = conv_even_odd(slot, sl, cg + sl * LANES)
            u_even, u_odd = conv_even_odd(slot, nslab + sl, cu + sl * LANES)
            ai_ref[slot, sl, pl.ds(0, half, stride=2), :] = gate(g_even, u_even)
            ai_ref[slot, sl, pl.ds(1, half, stride=2), :] = gate(g_odd, u_odd)
            a_ref[:, cg + sl * LANES:cg + (sl + 1) * LANES] = ai_ref[slot, sl].astype(BF16)

    ff = jnp.dot(a_ref[...], wdn_ref[...], preferred_element_type=F32)
    y_ref[...] = _layer_norm(ALPHA * x + ff, g_ref[...], b_ref[...])


def _ffn_ln(x1, seq_len, wup, cw, cb, wdn, g, b):
    n = x1.shape[0]
    tm = FFN_TM
    seq_tiles = seq_len // tm
    hb = tm // SUBLANES
    nhb = n // SUBLANES
    full = lambda a: pl.BlockSpec(a.shape, lambda i: (0,) * a.ndim, pipeline_mode=pl.Buffered(1))
    return pl.pallas_call(
        functools.partial(_ffn_kernel, seq_tiles),
        grid=(n // tm,),
        in_specs=[pl.BlockSpec((tm, D_MODEL), lambda i: (i, 0)),
                  pl.BlockSpec((SUBLANES, D_MODEL), lambda i: (jnp.maximum(i * hb - 1, 0), 0)),
                  pl.BlockSpec((SUBLANES, D_MODEL), lambda i: (jnp.minimum((i + 1) * hb, nhb - 1), 0)),
                  full(wup), full(cw), full(cb), full(wdn), full(g), full(b)],
        out_specs=pl.BlockSpec((tm, D_MODEL), lambda i: (i, 0)),
        out_shape=jax.ShapeDtypeStruct((n, D_MODEL), F32),
        scratch_shapes=[pltpu.VMEM((2, 2 * FFN_FC // LANES, tm + 2 * SUBLANES, LANES), F32),
                        pltpu.VMEM((2, FFN_FC // LANES, tm, LANES), F32),
                        pltpu.VMEM((tm, D_FF), BF16)],
        compiler_params=pltpu.CompilerParams(dimension_semantics=("parallel",),
                                             vmem_limit_bytes=VMEM_LIMIT),
        name="ffn_ln",
    )(x1, x1, x1, wup, cw, cb, wdn, g, b)


def kernel(x, positions, w_in, q_norm_g, w_q_b, kv_norm_g, w_kv_b, swa_sinks, w_o,
           ln1_g, ln1_b, w_up, conv_w, conv_b, w_down, ln2_g, ln2_b):
    b, s, d = x.shape
    n = b * s
    x2d = x.reshape(n, d)

    win_r, wq_r, wkv_r = _relayout_weights(w_in, w_q_b, w_kv_b)
    q, k, v, qs, ks4, vs2 = _projections(
        x2d, positions, win_r, q_norm_g.reshape(1, -1), wq_r, kv_norm_g.reshape(1, -1), wkv_r)

    o_mla = _mla_attention(q.reshape(b, s, -1), k.reshape(b, s, -1), v.reshape(b, s, -1))
    o_swa = _swa_attention(qs.reshape(b, s, -1), ks4.reshape(b, s, -1), vs2.reshape(b, s, -1),
                           positions, swa_sinks)

    n_mla = MLA_HEADS * V_DIM
    wo = w_o.astype(BF16)
    x1 = _oproj_ln(x2d, o_mla.reshape(n, -1), o_swa.reshape(n, -1), wo[:n_mla], wo[n_mla:],
                   ln1_g.reshape(1, -1), ln1_b.reshape(1, -1))

    y = _ffn_ln(x1, s, w_up.astype(BF16), conv_w.reshape(3, 2 * D_FF), conv_b.reshape(1, 2 * D_FF),
                w_down.astype(BF16), ln2_g.reshape(1, -1), ln2_b.reshape(1, -1))
    return y.reshape(b, s, d)
```

```python
import functools

import numpy as np
import jax
import jax.numpy as jnp
from jax import lax
from jax.experimental import pallas as pl
from jax.experimental.pallas import tpu as pltpu

F32 = jnp.float32
BF16 = jnp.bfloat16

D_MODEL = 1024
MLA_HEADS = 8
Q_LORA = 256
KV_LORA = 256
NOPE = 64
ROPE = 32
HALF_ROPE = ROPE // 2
V_DIM = 64
ROPE_THETA = 10000.0
SWA_HEADS = 8
SWA_KV_HEADS = 2
SWA_GROUP = SWA_HEADS // SWA_KV_HEADS
SWA_DIM = 64
WINDOW = 128
BAND = 128
D_FF = 2816
LN_EPS = 1e-5
RMS_EPS = 1e-6
DEPTH = 1
ALPHA = (2.0 * DEPTH) ** 0.25
LOG2_E = 1.4426950408889634
MASK_DIST = 1e30

LANES = 128
SUBLANES = 8
VMEM_LIMIT = 56 * 1024 * 1024

PROJ_TM = 512
SWA_TQ = 512
MLA_TQ = 512
MLA_TK = 256
MLA_VT_ROWS = V_DIM + 16
FFN_TM = 1024
HALO = 16
MIX_FFN_VMEM_LIMIT = 62 * 1024 * 1024
FFN_FC = 256
FFN_NCHUNK = D_FF // FFN_FC

_OFF_CQ = 0
_OFF_CKV = _OFF_CQ + Q_LORA
_OFF_QS = _OFF_CKV + KV_LORA
_OFF_KS = _OFF_QS + SWA_HEADS * SWA_DIM
_OFF_VS = _OFF_KS + SWA_KV_HEADS * SWA_DIM
_OFF_KR = _OFF_VS + SWA_KV_HEADS * SWA_DIM
IN_WIDTH_R = _OFF_KR + LANES


def _nt_dot(a, b):
    return lax.dot_general(a, b, (((1,), (1,)), ((), ())), preferred_element_type=F32)


def _rms(c, g):
    r = lax.rsqrt(jnp.mean(c * c, axis=-1, keepdims=True) + RMS_EPS)
    return c * r * g


def _layer_norm(y, g, b):
    mu = jnp.mean(y, axis=-1, keepdims=True)
    d = y - mu
    var = jnp.mean(d * d, axis=-1, keepdims=True)
    return d * lax.rsqrt(var + LN_EPS) * g + b


def _rope_tables(pos_row, invf_col):
    tm = pos_row.shape[1]
    ang = invf_col * pos_row
    cos_t = jnp.cos(ang)
    sin_t = jnp.sin(ang)
    pad = jnp.zeros((LANES - NOPE - ROPE, tm), F32)
    c_tab = jnp.concatenate([jnp.ones((NOPE, tm), F32), cos_t, cos_t, pad], axis=0).T
    s_tab = jnp.concatenate([jnp.zeros((NOPE, tm), F32), -sin_t, sin_t, pad], axis=0).T
    return c_tab, s_tab


def _proj_kernel(x_ref, pos_ref, invf_ref, win_ref, gq_ref, wq_ref, gkv_ref, wkv_ref,
                 q_ref, k_ref, v_ref, qs_ref, ks_ref, vs_ref):
    tm = x_ref.shape[0]
    xb = x_ref[...].astype(BF16)
    proj = jnp.dot(xb, win_ref[...], preferred_element_type=F32)
    c_tab, s_tab = _rope_tables(pos_ref[0].astype(F32), invf_ref[...])
    lane = lax.broadcasted_iota(jnp.int32, (tm, LANES), 1)
    lo = lane < SWA_DIM

    cqn = _rms(proj[:, _OFF_CQ:_OFF_CQ + Q_LORA], gq_ref[...]).astype(BF16)
    q = jnp.dot(cqn, wq_ref[...], preferred_element_type=F32)
    q_scale = (NOPE + ROPE) ** -0.5 * LOG2_E
    for h in range(MLA_HEADS):
        qh = q[:, h * LANES:(h + 1) * LANES]
        qsw = pltpu.roll(qh, LANES - ROPE, 1)
        q_ref[:, h * LANES:(h + 1) * LANES] = ((qh * c_tab + qsw * s_tab) * q_scale).astype(BF16)

    ckvn = _rms(proj[:, _OFF_CKV:_OFF_CKV + KV_LORA], gkv_ref[...]).astype(BF16)
    kv = jnp.dot(ckvn, wkv_ref[...], preferred_element_type=F32)
    kr_blk = proj[:, _OFF_KR:_OFF_KR + LANES]
    kr_sw = pltpu.roll(kr_blk, LANES // 2, 1)
    rope_lanes = (lane >= NOPE) & (lane < NOPE + ROPE)
    kr = jnp.where(rope_lanes, kr_blk * c_tab + kr_sw * s_tab, 0.0)
    for h in range(MLA_HEADS):
        k_ref[:, h * LANES:(h + 1) * LANES] = (kv[:, h * LANES:(h + 1) * LANES] + kr).astype(BF16)
    v_ref[...] = kv[:, MLA_HEADS * LANES:].astype(BF16)

    qs_scale = SWA_DIM ** -0.5 * LOG2_E
    qs_ref[...] = (proj[:, _OFF_QS:_OFF_QS + SWA_HEADS * SWA_DIM] * qs_scale).astype(BF16)
    ks = proj[:, _OFF_KS:_OFF_KS + LANES]
    ksr = pltpu.roll(ks, LANES // 2, 1)
    ks_ref[:, 0 * LANES:1 * LANES] = jnp.where(lo, ks, 0.0).astype(BF16)
    ks_ref[:, 1 * LANES:2 * LANES] = jnp.where(lo, 0.0, ksr).astype(BF16)
    ks_ref[:, 2 * LANES:3 * LANES] = jnp.where(lo, ksr, 0.0).astype(BF16)
    ks_ref[:, 3 * LANES:4 * LANES] = jnp.where(lo, 0.0, ks).astype(BF16)
    vs = proj[:, _OFF_VS:_OFF_VS + LANES]
    vsr = pltpu.roll(vs, LANES // 2, 1)
    vs_ref[:, 0 * LANES:1 * LANES] = jnp.where(lo, vs, 1.0).astype(BF16)
    vs_ref[:, 1 * LANES:2 * LANES] = jnp.where(lo, vsr, 1.0).astype(BF16)


def _relayout_weights(w_in, w_q_b, w_kv_b):
    d = w_in.shape[0]
    o = np.cumsum((Q_LORA, KV_LORA, ROPE, SWA_HEADS * SWA_DIM, SWA_KV_HEADS * SWA_DIM,
                   SWA_KV_HEADS * SWA_DIM)).tolist()
    w_cq, w_ckv, w_kr = w_in[:, :o[0]], w_in[:, o[0]:o[1]], w_in[:, o[1]:o[2]]
    w_qs, w_ks, w_vs = w_in[:, o[2]:o[3]], w_in[:, o[3]:o[4]], w_in[:, o[4]:o[5]]
    x1, x2 = w_kr[:, :HALF_ROPE], w_kr[:, HALF_ROPE:]
    z32 = jnp.zeros((d, LANES // 2 - ROPE), w_in.dtype)
    kr_blk = jnp.concatenate([x2, x1, z32, x1, x2, z32], axis=1)
    win_r = jnp.concatenate([w_cq, w_ckv, w_qs, w_ks, w_vs, kr_blk], axis=1).astype(BF16)

    wq = w_q_b.reshape(Q_LORA, MLA_HEADS, NOPE + ROPE)
    qn, q1, q2 = wq[..., :NOPE], wq[..., NOPE:NOPE + HALF_ROPE], wq[..., NOPE + HALF_ROPE:]
    wq_r = jnp.concatenate([qn, q1, q2, q2, q1], axis=-1).reshape(Q_LORA, MLA_HEADS * LANES).astype(BF16)

    wkv = w_kv_b.reshape(KV_LORA, MLA_HEADS, NOPE + V_DIM)
    kn = jnp.concatenate([wkv[..., :NOPE], jnp.zeros((KV_LORA, MLA_HEADS, LANES - NOPE), w_kv_b.dtype)], axis=-1)
    wkv_r = jnp.concatenate([kn.reshape(KV_LORA, MLA_HEADS * LANES),
                             wkv[..., NOPE:].reshape(KV_LORA, MLA_HEADS * V_DIM)], axis=1).astype(BF16)
    return win_r, wq_r, wkv_r


def _projections(x2d, positions, win_r, gq, wq_r, gkv, wkv_r):
    n = x2d.shape[0]
    tm = PROJ_TM
    inv_freq = ROPE_THETA ** (-jnp.arange(0, ROPE, 2, dtype=F32) / ROPE)
    invf_col = inv_freq.reshape(HALF_ROPE, 1)
    pos_rows = positions.reshape(n // tm, 1, tm)
    row = lambda w: pl.BlockSpec((tm, w), lambda i: (i, 0))
    full = lambda a: pl.BlockSpec(a.shape, lambda i: (0,) * a.ndim)
    widths = (MLA_HEADS * LANES, MLA_HEADS * LANES, MLA_HEADS * V_DIM,
              SWA_HEADS * SWA_DIM, 4 * LANES, 2 * LANES)
    return pl.pallas_call(
        _proj_kernel,
        grid=(n // tm,),
        in_specs=[row(D_MODEL), pl.BlockSpec((1, 1, tm), lambda i: (i, 0, 0)), full(invf_col),
                  full(win_r), full(gq), full(wq_r), full(gkv), full(wkv_r)],
        out_specs=[row(w) for w in widths],
        out_shape=[jax.ShapeDtypeStruct((n, w), BF16) for w in widths],
        compiler_params=pltpu.CompilerParams(dimension_semantics=("parallel",),
                                             vmem_limit_bytes=VMEM_LIMIT),
        name="proj",
    )(x2d, pos_rows, invf_col, win_r, gq, wq_r, gkv, wkv_r)


def _mla_kernel(q_ref, k_ref, v_ref, o_ref, vt_ref, st_ref, acc_ref):
    tq = q_ref.shape[1]
    tk = MLA_TK
    nc = k_ref.shape[1] // tk

    @pl.when(pl.program_id(2) == 0)
    def _():
        ones = jnp.ones((MLA_VT_ROWS - V_DIM, tk), BF16)
        for c in range(nc):
            vt = v_ref[0, c * tk:(c + 1) * tk, :].astype(F32).T.astype(BF16)
            for hh in range(2):
                vt_ref[c, hh, 0:V_DIM, :] = vt[hh * V_DIM:(hh + 1) * V_DIM]
                vt_ref[c, hh, V_DIM:MLA_VT_ROWS, :] = ones

    outs = []
    for hh in range(2):
        q = q_ref[0, :, hh * LANES:(hh + 1) * LANES]

        def scores(c, slot):
            st_ref[slot] = _nt_dot(k_ref[0, c * tk:(c + 1) * tk, hh * LANES:(hh + 1) * LANES], q)

        def consume(c, slot, m):
            st = st_ref[slot]
            m_new = jnp.maximum(m, jnp.max(st, axis=0, keepdims=True))
            p = jnp.exp2(st - m_new).astype(BF16)
            pv = jnp.dot(vt_ref[c, hh], p, preferred_element_type=F32)
            acc_ref[...] = jnp.exp2(m - m_new) * acc_ref[...] + pv
            return m_new

        acc_ref[...] = jnp.zeros_like(acc_ref)
        scores(0, 0)
        scores(1, 1)
        m = jnp.full((1, tq), -jnp.inf, F32)
        for j in range(nc // 2):
            cur = 2 * (j % 2)
            if j + 1 < nc // 2:
                scores(2 * j + 2, 2 - cur)
                scores(2 * j + 3, 3 - cur)
            m = consume(2 * j, cur, m)
            m = consume(2 * j + 1, cur + 1, m)
        outs.append(acc_ref[0:V_DIM, :] / acc_ref[V_DIM:V_DIM + 1, :])
    o_ref[0] = jnp.concatenate(outs, axis=0).T.astype(BF16)


def _mla_attention(q, k, v):
    b, s, _ = q.shape
    tq = MLA_TQ
    return pl.pallas_call(
        _mla_kernel,
        grid=(b, MLA_HEADS // 2, s // tq),
        in_specs=[pl.BlockSpec((1, tq, 2 * LANES), lambda bi, hp, i: (bi, i, hp)),
                  pl.BlockSpec((1, s, 2 * LANES), lambda bi, hp, i: (bi, 0, hp)),
                  pl.BlockSpec((1, s, LANES), lambda bi, hp, i: (bi, 0, hp))],
        out_specs=pl.BlockSpec((1, tq, LANES), lambda bi, hp, i: (bi, i, hp)),
        out_shape=jax.ShapeDtypeStruct((b, s, MLA_HEADS * V_DIM), BF16),
        scratch_shapes=[pltpu.VMEM((s // MLA_TK, 2, MLA_VT_ROWS, MLA_TK), BF16),
                        pltpu.VMEM((4, MLA_TK, tq), F32),
                        pltpu.VMEM((MLA_VT_ROWS, tq), F32)],
        compiler_params=pltpu.CompilerParams(
            dimension_semantics=("parallel", "parallel", "arbitrary"),
            vmem_limit_bytes=VMEM_LIMIT),
        name="mla_attn",
    )(q, k, v)


def _alibi_slopes(n_heads):
    return 2.0 ** (-8.0 * (np.arange(n_heads, dtype=np.float32) + 1.0) / n_heads)


def _swa_kernel(sink_ref, qs_ref, ks_ref, vs_ref, pcol_ref, prow_ref, o_ref):
    nb = prow_ref.shape[1]
    sub = qs_ref.shape[1] // BAND
    win = 3 * BAND
    qi = lax.broadcasted_iota(jnp.int32, (BAND, win), 0)
    ci = lax.broadcasted_iota(jnp.int32, (BAND, win), 1)
    lane = lax.broadcasted_iota(jnp.int32, (BAND, LANES), 1)
    lo = lane < SWA_DIM
    slopes = _alibi_slopes(SWA_HEADS)

    for jb in range(sub):
        i = pl.program_id(1) * sub + jb
        r0 = jb * BAND
        start_blk = jnp.clip(i - 1, 0, nb - 3)
        start = pl.multiple_of(start_blk * BAND, BAND)
        pq = pcol_ref[0, pl.ds(r0, BAND), :].astype(F32)
        pk = jnp.concatenate([prow_ref[0, pl.ds(start_blk + j, 1), :] for j in range(3)],
                             axis=-1).astype(F32)
        rel = ci - qi + (start - i * BAND)
        dist = jnp.where(jnp.abs(rel) <= WINDOW, jnp.abs(pq - pk), MASK_DIST)
        for kvh in range(SWA_KV_HEADS):
            lhs = jnp.concatenate(
                [qs_ref[0, pl.ds(r0, BAND), (2 * kvh) * LANES:(2 * kvh + 1) * LANES],
                 qs_ref[0, pl.ds(r0, BAND), (2 * kvh + 1) * LANES:(2 * kvh + 2) * LANES]], axis=0)
            k_lo = ks_ref[0, pl.ds(start, win), (2 * kvh) * LANES:(2 * kvh + 1) * LANES]
            k_hi = ks_ref[0, pl.ds(start, win), (2 * kvh + 1) * LANES:(2 * kvh + 2) * LANES]
            s_all = _nt_dot(lhs, jnp.concatenate([k_lo, k_hi], axis=0))
            vv = vs_ref[0, pl.ds(start, win), kvh * LANES:(kvh + 1) * LANES]
            es, sinks = [], []
            for g in range(SWA_GROUP):
                h = kvh * SWA_GROUP + g
                pair, par = g // 2, g % 2
                s = s_all[pair * BAND:(pair + 1) * BAND, par * win:(par + 1) * win]
                s = s - float(slopes[h] * LOG2_E) * dist
                sk = sink_ref[h] * LOG2_E
                m = jnp.maximum(jnp.max(s, axis=-1, keepdims=True), sk)
                sinks.append(jnp.exp2(sk - m))
                es.append(jnp.exp2(s - m).astype(BF16))
            o = jnp.dot(jnp.concatenate(es, axis=0), vv, preferred_element_type=F32)
            res = []
            for g in range(SWA_GROUP):
                og = o[g * BAND:(g + 1) * BAND]
                den = jnp.where(lo, pltpu.roll(og, LANES // 2, 1), og) + sinks[g]
                res.append(og / den)
            for pair in range(2):
                col = (kvh * 2 + pair) * LANES
                o_ref[0, pl.ds(r0, BAND), col:col + LANES] = jnp.where(
                    lo, res[2 * pair], pltpu.roll(res[2 * pair + 1], LANES // 2, 1)).astype(BF16)


def _swa_attention(qs, ks4, vs2, positions, sinks):
    b, s, _ = qs.shape
    nb = s // BAND
    tq = SWA_TQ
    pcol = positions.reshape(b, s, 1)
    prow = positions.reshape(b, nb, BAND)
    return pl.pallas_call(
        _swa_kernel,
        grid=(b, s // tq),
        in_specs=[pl.BlockSpec(memory_space=pltpu.SMEM),
                  pl.BlockSpec((1, tq, SWA_HEADS * SWA_DIM), lambda bi, i: (bi, i, 0)),
                  pl.BlockSpec((1, s, 4 * LANES), lambda bi, i: (bi, 0, 0)),
                  pl.BlockSpec((1, s, 2 * LANES), lambda bi, i: (bi, 0, 0)),
                  pl.BlockSpec((1, tq, 1), lambda bi, i: (bi, i, 0)),
                  pl.BlockSpec((1, nb, BAND), lambda bi, i: (bi, 0, 0))],
        out_specs=pl.BlockSpec((1, tq, SWA_HEADS * SWA_DIM), lambda bi, i: (bi, i, 0)),
        out_shape=jax.ShapeDtypeStruct((b, s, SWA_HEADS * SWA_DIM), BF16),
        compiler_params=pltpu.CompilerParams(dimension_semantics=("parallel", "parallel"),
                                             vmem_limit_bytes=VMEM_LIMIT),
        name="swa_attn",
    )(sinks.astype(F32), qs, ks4, vs2, pcol, prow)


def _mix_ffn_kernel(seq_tiles, x_ref, xp_ref, xn_ref, om_ref, omp_ref, omn_ref, os_ref, osp_ref,
                    osn_ref, wo1_ref, wo2_ref, g1_ref, b1_ref, wup_ref, cw_ref, cb_ref, wdn_ref,
                    g2_ref, b2_ref, y_ref, h_ref, ai_ref, a_ref):
    tm = x_ref.shape[0]
    fc = FFN_FC
    half = tm // 2
    nslab = fc // LANES
    rows = tm + 2 * HALO
    i = pl.program_id(0)
    t = i % seq_tiles

    def ext(main_ref, prev_ref, next_ref):
        return jnp.concatenate([prev_ref[...], main_ref[...], next_ref[...]], axis=0)

    mix = jnp.dot(ext(om_ref, omp_ref, omn_ref), wo1_ref[...], preferred_element_type=F32)
    mix = mix + jnp.dot(ext(os_ref, osp_ref, osn_ref), wo2_ref[...], preferred_element_type=F32)
    x1e = _layer_norm(ALPHA * ext(x_ref, xp_ref, xn_ref) + mix, g1_ref[...], b1_ref[...])
    r = lax.broadcasted_iota(jnp.int32, (rows, 1), 0)
    outside = ((t == 0) & (r < HALO)) | ((t == seq_tiles - 1) & (r >= HALO + tm))
    x1e = jnp.where(outside, 0.0, x1e)
    x1 = x1e[HALO:HALO + tm]
    xe = x1e.astype(BF16)

    def conv_even_odd(slot, slab, col):
        e = [h_ref[slot, slab, pl.ds(HALO - 1 + k, half, stride=2), :] for k in range(4)]
        w = cw_ref[:, col:col + LANES]
        bias = cb_ref[:, col:col + LANES]
        even = e[0] * w[0:1] + e[1] * w[1:2] + e[2] * w[2:3] + bias
        odd = e[1] * w[0:1] + e[2] * w[1:2] + e[3] * w[2:3] + bias
        return even, odd

    def gate(gt, ut):
        return 0.5 * gt * (1.0 + lax.erf(gt * (2.0 ** -0.5))) * ut

    for j in range(FFN_NCHUNK):
        slot = j % 2
        cg, cu = j * fc, D_FF + j * fc
        hg = jnp.dot(xe, wup_ref[:, cg:cg + fc], preferred_element_type=F32)
        hu = jnp.dot(xe, wup_ref[:, cu:cu + fc], preferred_element_type=F32)
        for sl in range(nslab):
            h_ref[slot, sl] = hg[:, sl * LANES:(sl + 1) * LANES]
            h_ref[slot, nslab + sl] = hu[:, sl * LANES:(sl + 1) * LANES]
        for sl in range(nslab):
            g_even, g_odd = conv_even_odd(slot, sl, cg + sl * LANES)
            u_even, u_odd = conv_even_odd(slot, nslab + sl, cu + sl * LANES)
            ai_ref[slot, sl, pl.ds(0, half, stride=2), :] = gate(g_even, u_even)
            ai_ref[slot, sl, pl.ds(1, half, stride=2), :] = gate(g_odd, u_odd)
            a_ref[:, cg + sl * LANES:cg + (sl + 1) * LANES] = ai_ref[slot, sl].astype(BF16)

    ff = jnp.dot(a_ref[...], wdn_ref[...], preferred_element_type=F32)
    y_ref[...] = _layer_norm(ALPHA * x1 + ff, g2_ref[...], b2_ref[...])


def _mix_ffn(x2d, om, osw, seq_len, wo1, wo2, g1, b1, wup, cw, cb, wdn, g2, b2):
    n = x2d.shape[0]
    tm = FFN_TM
    seq_tiles = seq_len // tm
    hb = tm // HALO
    nhb = n // HALO
    full = lambda a: pl.BlockSpec(a.shape, lambda i: (0,) * a.ndim, pipeline_mode=pl.Buffered(1))

    def tile_and_halos(width):
        return [pl.BlockSpec((tm, width), lambda i: (i, 0)),
                pl.BlockSpec((HALO, width), lambda i: (jnp.maximum(i * hb - 1, 0), 0)),
                pl.BlockSpec((HALO, width), lambda i: (jnp.minimum((i + 1) * hb, nhb - 1), 0))]

    weights = (wo1, wo2, g1, b1, wup, cw, cb, wdn, g2, b2)
    return pl.pallas_call(
        functools.partial(_mix_ffn_kernel, seq_tiles),
        grid=(n // tm,),
        in_specs=(tile_and_halos(D_MODEL) + tile_and_halos(om.shape[1]) + tile_and_halos(osw.shape[1])
                  + [full(w) for w in weights]),
        out_specs=pl.BlockSpec((tm, D_MODEL), lambda i: (i, 0)),
        out_shape=jax.ShapeDtypeStruct((n, D_MODEL), F32),
        scratch_shapes=[pltpu.VMEM((2, 2 * FFN_FC // LANES, tm + 2 * HALO, LANES), F32),
                        pltpu.VMEM((2, FFN_FC // LANES, tm, LANES), F32),
                        pltpu.VMEM((tm, D_FF), BF16)],
        compiler_params=pltpu.CompilerParams(dimension_semantics=("parallel",),
                                             vmem_limit_bytes=MIX_FFN_VMEM_LIMIT),
        name="mix_ffn",
    )(x2d, x2d, x2d, om, om, om, osw, osw, osw, *weights)


def kernel(x, positions, w_in, q_norm_g, w_q_b, kv_norm_g, w_kv_b, swa_sinks, w_o,
           ln1_g, ln1_b, w_up, conv_w, conv_b, w_down, ln2_g, ln2_b):
    b, s, d = x.shape
    n = b * s
    x2d = x.reshape(n, d)

    win_r, wq_r, wkv_r = _relayout_weights(w_in, w_q_b, w_kv_b)
    q, k, v, qs, ks4, vs2 = _projections(
        x2d, positions, win_r, q_norm_g.reshape(1, -1), wq_r, kv_norm_g.reshape(1, -1), wkv_r)

    o_mla = _mla_attention(q.reshape(b, s, -1), k.reshape(b, s, -1), v.reshape(b, s, -1))
    o_swa = _swa_attention(qs.reshape(b, s, -1), ks4.reshape(b, s, -1), vs2.reshape(b, s, -1),
                           positions, swa_sinks)

    n_mla = MLA_HEADS * V_DIM
    wo = w_o.astype(BF16)
    y = _mix_ffn(x2d, o_mla.reshape(n, -1), o_swa.reshape(n, -1), s, wo[:n_mla], wo[n_mla:],
                 ln1_g.reshape(1, -1), ln1_b.reshape(1, -1),
                 w_up.astype(BF16), conv_w.reshape(3, 2 * D_FF), conv_b.reshape(1, 2 * D_FF),
                 w_down.astype(BF16), ln2_g.reshape(1, -1), ln2_b.reshape(1, -1))
    return y.reshape(b, s, d)
```

```python
import functools

import numpy as np
import jax
import jax.numpy as jnp
from jax import lax
from jax.experimental import pallas as pl
from jax.experimental.pallas import tpu as pltpu

F32 = jnp.float32
BF16 = jnp.bfloat16

D_MODEL = 1024
MLA_HEADS = 8
Q_LORA = 256
KV_LORA = 256
NOPE = 64
ROPE = 32
HALF_ROPE = ROPE // 2
V_DIM = 64
ROPE_THETA = 10000.0
SWA_HEADS = 8
SWA_KV_HEADS = 2
SWA_GROUP = SWA_HEADS // SWA_KV_HEADS
SWA_DIM = 64
WINDOW = 128
BAND = 128
D_FF = 2816
LN_EPS = 1e-5
RMS_EPS = 1e-6
DEPTH = 1
ALPHA = (2.0 * DEPTH) ** 0.25
LOG2_E = 1.4426950408889634
MASK_DIST = 1e30

LANES = 128
SUBLANES = 8
VMEM_LIMIT = 56 * 1024 * 1024

PROJ_TM = 512
SWA_TQ = 512
SWA_VT_ROWS = SWA_DIM + 16
MLA_TQ = 512
MLA_TK = 256
MLA_VT_ROWS = V_DIM + 16
FFN_TM = 1024
LN_ROWS = 256
HALO = 16
MIX_FFN_VMEM_LIMIT = 62 * 1024 * 1024
FFN_FC = 256
FFN_NCHUNK = D_FF // FFN_FC

_OFF_CQ = 0
_OFF_CKV = _OFF_CQ + Q_LORA
_OFF_QS = _OFF_CKV + KV_LORA
_OFF_KS = _OFF_QS + SWA_HEADS * SWA_DIM
_OFF_VS = _OFF_KS + SWA_KV_HEADS * SWA_DIM
_OFF_KR = _OFF_VS + SWA_KV_HEADS * SWA_DIM
IN_WIDTH_R = _OFF_KR + LANES


def _nt_dot(a, b):
    return lax.dot_general(a, b, (((1,), (1,)), ((), ())), preferred_element_type=F32)


def _rms(c, g):
    r = lax.rsqrt(jnp.mean(c * c, axis=-1, keepdims=True) + RMS_EPS)
    return c * r * g


def _layer_norm(y, g, b):
    mu = jnp.mean(y, axis=-1, keepdims=True)
    d = y - mu
    var = jnp.mean(d * d, axis=-1, keepdims=True)
    return d * lax.rsqrt(var + LN_EPS) * g + b


def _rope_tables(pos_row, invf_col):
    tm = pos_row.shape[1]
    ang = invf_col * pos_row
    cos_t = jnp.cos(ang)
    sin_t = jnp.sin(ang)
    pad = jnp.zeros((LANES - NOPE - ROPE, tm), F32)
    c_tab = jnp.concatenate([jnp.ones((NOPE, tm), F32), cos_t, cos_t, pad], axis=0).T
    s_tab = jnp.concatenate([jnp.zeros((NOPE, tm), F32), -sin_t, sin_t, pad], axis=0).T
    return c_tab, s_tab


def _proj_kernel(x_ref, pos_ref, invf_ref, win_ref, gq_ref, wq_ref, gkv_ref, wkv_ref,
                 q_ref, k_ref, v_ref, qs_ref, ks_ref, vs_ref):
    tm = x_ref.shape[0]
    xb = x_ref[...].astype(BF16)
    proj = jnp.dot(xb, win_ref[...], preferred_element_type=F32)
    c_tab, s_tab = _rope_tables(pos_ref[0].astype(F32), invf_ref[...])
    lane = lax.broadcasted_iota(jnp.int32, (tm, LANES), 1)
    lo = lane < SWA_DIM

    cqn = _rms(proj[:, _OFF_CQ:_OFF_CQ + Q_LORA], gq_ref[...]).astype(BF16)
    q = jnp.dot(cqn, wq_ref[...], preferred_element_type=F32)
    q_scale = (NOPE + ROPE) ** -0.5 * LOG2_E
    for h in range(MLA_HEADS):
        qh = q[:, h * LANES:(h + 1) * LANES]
        qsw = pltpu.roll(qh, LANES - ROPE, 1)
        q_ref[:, h * LANES:(h + 1) * LANES] = ((qh * c_tab + qsw * s_tab) * q_scale).astype(BF16)

    ckvn = _rms(proj[:, _OFF_CKV:_OFF_CKV + KV_LORA], gkv_ref[...]).astype(BF16)
    kv = jnp.dot(ckvn, wkv_ref[...], preferred_element_type=F32)
    kr_blk = proj[:, _OFF_KR:_OFF_KR + LANES]
    kr_sw = pltpu.roll(kr_blk, LANES // 2, 1)
    rope_lanes = (lane >= NOPE) & (lane < NOPE + ROPE)
    kr = jnp.where(rope_lanes, kr_blk * c_tab + kr_sw * s_tab, 0.0)
    for h in range(MLA_HEADS):
        k_ref[:, h * LANES:(h + 1) * LANES] = (kv[:, h * LANES:(h + 1) * LANES] + kr).astype(BF16)
    v_ref[...] = kv[:, MLA_HEADS * LANES:].astype(BF16)

    qs_scale = SWA_DIM ** -0.5 * LOG2_E
    qs_ref[...] = (proj[:, _OFF_QS:_OFF_QS + SWA_HEADS * SWA_DIM] * qs_scale).astype(BF16)
    ks = proj[:, _OFF_KS:_OFF_KS + LANES]
    ksr = pltpu.roll(ks, LANES // 2, 1)
    ks_ref[:, 0 * LANES:1 * LANES] = jnp.where(lo, ks, 0.0).astype(BF16)
    ks_ref[:, 1 * LANES:2 * LANES] = jnp.where(lo, 0.0, ksr).astype(BF16)
    ks_ref[:, 2 * LANES:3 * LANES] = jnp.where(lo, ksr, 0.0).astype(BF16)
    ks_ref[:, 3 * LANES:4 * LANES] = jnp.where(lo, 0.0, ks).astype(BF16)
    vs = proj[:, _OFF_VS:_OFF_VS + LANES]
    vsr = pltpu.roll(vs, LANES // 2, 1)
    vs_ref[:, 0 * LANES:1 * LANES] = jnp.where(lo, vs, 1.0).astype(BF16)
    vs_ref[:, 1 * LANES:2 * LANES] = jnp.where(lo, vsr, 1.0).astype(BF16)


def _relayout_weights(w_in, w_q_b, w_kv_b):
    d = w_in.shape[0]
    o = np.cumsum((Q_LORA, KV_LORA, ROPE, SWA_HEADS * SWA_DIM, SWA_KV_HEADS * SWA_DIM,
                   SWA_KV_HEADS * SWA_DIM)).tolist()
    w_cq, w_ckv, w_kr = w_in[:, :o[0]], w_in[:, o[0]:o[1]], w_in[:, o[1]:o[2]]
    w_qs, w_ks, w_vs = w_in[:, o[2]:o[3]], w_in[:, o[3]:o[4]], w_in[:, o[4]:o[5]]
    x1, x2 = w_kr[:, :HALF_ROPE], w_kr[:, HALF_ROPE:]
    z32 = jnp.zeros((d, LANES // 2 - ROPE), w_in.dtype)
    kr_blk = jnp.concatenate([x2, x1, z32, x1, x2, z32], axis=1)
    win_r = jnp.concatenate([w_cq, w_ckv, w_qs, w_ks, w_vs, kr_blk], axis=1).astype(BF16)

    wq = w_q_b.reshape(Q_LORA, MLA_HEADS, NOPE + ROPE)
    qn, q1, q2 = wq[..., :NOPE], wq[..., NOPE:NOPE + HALF_ROPE], wq[..., NOPE + HALF_ROPE:]
    wq_r = jnp.concatenate([qn, q1, q2, q2, q1], axis=-1).reshape(Q_LORA, MLA_HEADS * LANES).astype(BF16)

    wkv = w_kv_b.reshape(KV_LORA, MLA_HEADS, NOPE + V_DIM)
    kn = jnp.concatenate([wkv[..., :NOPE], jnp.zeros((KV_LORA, MLA_HEADS, LANES - NOPE), w_kv_b.dtype)], axis=-1)
    wkv_r = jnp.concatenate([kn.reshape(KV_LORA, MLA_HEADS * LANES),
                             wkv[..., NOPE:].reshape(KV_LORA, MLA_HEADS * V_DIM)], axis=1).astype(BF16)
    return win_r, wq_r, wkv_r


def _projections(x2d, positions, win_r, gq, wq_r, gkv, wkv_r):
    n = x2d.shape[0]
    tm = PROJ_TM
    inv_freq = ROPE_THETA ** (-jnp.arange(0, ROPE, 2, dtype=F32) / ROPE)
    invf_col = inv_freq.reshape(HALF_ROPE, 1)
    pos_rows = positions.reshape(n // tm, 1, tm)
    row = lambda w: pl.BlockSpec((tm, w), lambda i: (i, 0))
    full = lambda a: pl.BlockSpec(a.shape, lambda i: (0,) * a.ndim)
    widths = (MLA_HEADS * LANES, MLA_HEADS * LANES, MLA_HEADS * V_DIM,
              SWA_HEADS * SWA_DIM, 4 * LANES, 2 * LANES)
    return pl.pallas_call(
        _proj_kernel,
        grid=(n // tm,),
        in_specs=[row(D_MODEL), pl.BlockSpec((1, 1, tm), lambda i: (i, 0, 0)), full(invf_col),
                  full(win_r), full(gq), full(wq_r), full(gkv), full(wkv_r)],
        out_specs=[row(w) for w in widths],
        out_shape=[jax.ShapeDtypeStruct((n, w), BF16) for w in widths],
        compiler_params=pltpu.CompilerParams(dimension_semantics=("parallel",),
                                             vmem_limit_bytes=VMEM_LIMIT),
        name="proj",
    )(x2d, pos_rows, invf_col, win_r, gq, wq_r, gkv, wkv_r)


def _mla_body(q_ref, k_ref, v_ref, o_ref, vt_ref, st_ref, acc_ref):
    tq = q_ref.shape[1]
    tk = MLA_TK
    nc = k_ref.shape[1] // tk

    @pl.when(pl.program_id(2) == 0)
    def _():
        ones = jnp.ones((MLA_VT_ROWS - V_DIM, tk), BF16)
        for c in range(nc):
            vt = v_ref[0, c * tk:(c + 1) * tk, :].astype(F32).T.astype(BF16)
            for hh in range(2):
                vt_ref[c, hh, 0:V_DIM, :] = vt[hh * V_DIM:(hh + 1) * V_DIM]
                vt_ref[c, hh, V_DIM:MLA_VT_ROWS, :] = ones

    outs = []
    for hh in range(2):
        q = q_ref[0, :, hh * LANES:(hh + 1) * LANES]

        def scores(c, slot):
            st_ref[slot] = _nt_dot(k_ref[0, c * tk:(c + 1) * tk, hh * LANES:(hh + 1) * LANES], q)

        def consume(c, slot, m):
            st = st_ref[slot]
            m_new = jnp.maximum(m, jnp.max(st, axis=0, keepdims=True))
            p = jnp.exp2(st - m_new).astype(BF16)
            pv = jnp.dot(vt_ref[c, hh], p, preferred_element_type=F32)
            acc_ref[...] = jnp.exp2(m - m_new) * acc_ref[...] + pv
            return m_new

        acc_ref[...] = jnp.zeros_like(acc_ref)
        scores(0, 0)
        scores(1, 1)
        m = jnp.full((1, tq), -jnp.inf, F32)
        for j in range(nc // 2):
            cur = 2 * (j % 2)
            if j + 1 < nc // 2:
                scores(2 * j + 2, 2 - cur)
                scores(2 * j + 3, 3 - cur)
            m = consume(2 * j, cur, m)
            m = consume(2 * j + 1, cur + 1, m)
        outs.append(acc_ref[0:V_DIM, :] / acc_ref[V_DIM:V_DIM + 1, :])
    o_ref[0] = jnp.concatenate(outs, axis=0).T.astype(BF16)


def _mla_attention(q, k, v):
    b, s, _ = q.shape
    tq = MLA_TQ
    return pl.pallas_call(
        _mla_body,
        grid=(b, MLA_HEADS // 2, s // tq),
        in_specs=[pl.BlockSpec((1, tq, 2 * LANES), lambda bi, hp, i: (bi, i, hp)),
                  pl.BlockSpec((1, s, 2 * LANES), lambda bi, hp, i: (bi, 0, hp)),
                  pl.BlockSpec((1, s, LANES), lambda bi, hp, i: (bi, 0, hp))],
        out_specs=pl.BlockSpec((1, tq, LANES), lambda bi, hp, i: (bi, i, hp)),
        out_shape=jax.ShapeDtypeStruct((b, s, MLA_HEADS * V_DIM), BF16),
        scratch_shapes=[pltpu.VMEM((s // MLA_TK, 2, MLA_VT_ROWS, MLA_TK), BF16),
                        pltpu.VMEM((4, MLA_TK, tq), F32),
                        pltpu.VMEM((MLA_VT_ROWS, tq), F32)],
        compiler_params=pltpu.CompilerParams(
            dimension_semantics=("parallel", "parallel", "arbitrary"),
            vmem_limit_bytes=VMEM_LIMIT),
        name="mla_attn",
    )(q, k, v)


def _alibi_slopes(n_heads):
    return 2.0 ** (-8.0 * (np.arange(n_heads, dtype=np.float32) + 1.0) / n_heads)


def _swa_body(first_of_batch, blocks, sink_ref, qs_ref, ks_ref, vs_ref, pcol_ref, prow_ref, o_ref, vt_ref):
    nb = prow_ref.shape[1]
    win = 3 * BAND
    ki = lax.broadcasted_iota(jnp.int32, (win, BAND), 0)
    qi = lax.broadcasted_iota(jnp.int32, (win, BAND), 1)
    slopes = _alibi_slopes(SWA_HEADS)

    @pl.when(first_of_batch)
    def _():
        for kvh in range(SWA_KV_HEADS):
            for blk in range(nb):
                vt = vs_ref[0, blk * BAND:(blk + 1) * BAND, kvh * LANES:(kvh + 1) * LANES]
                vt_ref[kvh, blk] = vt.astype(F32).T[0:SWA_VT_ROWS].astype(BF16)

    for i, r0 in blocks:
        start_blk = jnp.clip(i - 1, 0, nb - 3)
        start = pl.multiple_of(start_blk * BAND, BAND)
        pk = pcol_ref[0, pl.ds(start, win), :].astype(F32)
        pq = prow_ref[0, pl.ds(i, 1), :].astype(F32)
        rel = ki - qi + (start - i * BAND)
        dist = jnp.where(jnp.abs(rel) <= WINDOW, jnp.abs(pk - pq), MASK_DIST)
        for kvh in range(SWA_KV_HEADS):
            q_pairs = jnp.concatenate(
                [qs_ref[0, r0:r0 + BAND, (2 * kvh) * LANES:(2 * kvh + 1) * LANES],
                 qs_ref[0, r0:r0 + BAND, (2 * kvh + 1) * LANES:(2 * kvh + 2) * LANES]], axis=0)
            k_lo = ks_ref[0, pl.ds(start, win), (2 * kvh) * LANES:(2 * kvh + 1) * LANES]
            k_hi = ks_ref[0, pl.ds(start, win), (2 * kvh + 1) * LANES:(2 * kvh + 2) * LANES]
            s_par = (_nt_dot(k_lo, q_pairs), _nt_dot(k_hi, q_pairs))
            es, sink_w = [], []
            for g in range(SWA_GROUP):
                h = kvh * SWA_GROUP + g
                pair, par = g // 2, g % 2
                s = s_par[par][:, pair * BAND:(pair + 1) * BAND]
                s = s - float(slopes[h] * LOG2_E) * dist
                sk = sink_ref[h] * LOG2_E
                m = jnp.maximum(jnp.max(s, axis=0, keepdims=True), sk)
                sink_w.append(jnp.exp2(sk - m))
                es.append(jnp.exp2(s - m).astype(BF16))
            vt_win = jnp.concatenate([vt_ref[kvh, start_blk + j] for j in range(3)], axis=1)
            ot = jnp.dot(vt_win, jnp.concatenate(es, axis=1), preferred_element_type=F32)
            for pair in range(2):
                heads = []
                for par in range(2):
                    g = 2 * pair + par
                    og = ot[:, g * BAND:(g + 1) * BAND]
                    heads.append(og[0:SWA_DIM] / (og[SWA_DIM:SWA_DIM + 1] + sink_w[g]))
                col = (kvh * 2 + pair) * LANES
                o_ref[0, r0:r0 + BAND, col:col + LANES] = jnp.concatenate(heads, axis=0).T.astype(BF16)


def _swa_kernel(sink_ref, qs_ref, ks_ref, vs_ref, pcol_ref, prow_ref, o_ref, vt_ref):
    sub = qs_ref.shape[1] // BAND
    step = pl.program_id(1)
    _swa_body(step == 0, [(step * sub + jb, jb * BAND) for jb in range(sub)],
              sink_ref, qs_ref, ks_ref, vs_ref, pcol_ref, prow_ref, o_ref, vt_ref)


def _swa_attention(qs, ks4, vs2, positions, sinks):
    b, s, _ = qs.shape
    nb = s // BAND
    tq = SWA_TQ
    pcol = positions.reshape(b, s, 1)
    prow = positions.reshape(b, nb, BAND)
    return pl.pallas_call(
        _swa_kernel,
        grid=(b, s // tq),
        in_specs=[pl.BlockSpec(memory_space=pltpu.SMEM),
                  pl.BlockSpec((1, tq, SWA_HEADS * SWA_DIM), lambda bi, i: (bi, i, 0)),
                  pl.BlockSpec((1, s, 4 * LANES), lambda bi, i: (bi, 0, 0)),
                  pl.BlockSpec((1, s, 2 * LANES), lambda bi, i: (bi, 0, 0)),
                  pl.BlockSpec((1, s, 1), lambda bi, i: (bi, 0, 0)),
                  pl.BlockSpec((1, nb, BAND), lambda bi, i: (bi, 0, 0))],
        out_specs=pl.BlockSpec((1, tq, SWA_HEADS * SWA_DIM), lambda bi, i: (bi, i, 0)),
        out_shape=jax.ShapeDtypeStruct((b, s, SWA_HEADS * SWA_DIM), BF16),
        scratch_shapes=[pltpu.VMEM((SWA_KV_HEADS, nb, SWA_VT_ROWS, BAND), BF16)],
        compiler_params=pltpu.CompilerParams(dimension_semantics=("parallel", "arbitrary"),
                                             vmem_limit_bytes=VMEM_LIMIT),
        name="swa_attn",
    )(sinks.astype(F32), qs, ks4, vs2, pcol, prow)


def _mix_ffn_kernel(seq_tiles, x_ref, xp_ref, xn_ref, om_ref, omp_ref, omn_ref, os_ref, osp_ref,
                    osn_ref, wo1_ref, wo2_ref, g1_ref, b1_ref, wup_ref, cw_ref, cb_ref, wdn_ref,
                    g2_ref, b2_ref, y_ref, h_ref, ai_ref, a_ref):
    tm = x_ref.shape[0]
    fc = FFN_FC
    half = tm // 2
    nslab = fc // LANES
    rows = tm + 2 * HALO
    i = pl.program_id(0)
    t = i % seq_tiles

    nblk = tm // LN_ROWS

    def ext(r, main_ref, prev_ref, next_ref):
        parts = [main_ref[r * LN_ROWS:(r + 1) * LN_ROWS, :]]
        if r == 0:
            parts.insert(0, prev_ref[...])
        if r == nblk - 1:
            parts.append(next_ref[...])
        return parts[0] if len(parts) == 1 else jnp.concatenate(parts, axis=0)

    x1e_blocks = []
    for r in range(nblk):
        mix = jnp.dot(ext(r, om_ref, omp_ref, omn_ref), wo1_ref[...], preferred_element_type=F32)
        mix = mix + jnp.dot(ext(r, os_ref, osp_ref, osn_ref), wo2_ref[...], preferred_element_type=F32)
        blk = _layer_norm(ALPHA * ext(r, x_ref, xp_ref, xn_ref) + mix, g1_ref[...], b1_ref[...])
        row = lax.broadcasted_iota(jnp.int32, (blk.shape[0], 1), 0)
        if r == 0:
            blk = jnp.where((t == 0) & (row < HALO), 0.0, blk)
        if r == nblk - 1:
            first_next = blk.shape[0] - HALO
            blk = jnp.where((t == seq_tiles - 1) & (row >= first_next), 0.0, blk)
        x1e_blocks.append(blk)
    xe_blocks = [blk.astype(BF16) for blk in x1e_blocks]
    xe = jnp.concatenate(xe_blocks, axis=0)
    x1_blocks = list(x1e_blocks)
    x1_blocks[0] = x1_blocks[0][HALO:]
    x1_blocks[-1] = x1_blocks[-1][:x1_blocks[-1].shape[0] - HALO]

    def up_proj(j, cols):
        w = wup_ref[:, cols:cols + fc]
        if j == 0:
            return jnp.concatenate([jnp.dot(xb, w, preferred_element_type=F32) for xb in xe_blocks], axis=0)
        return jnp.dot(xe, w, preferred_element_type=F32)

    def conv_even_odd(slot, slab, col):
        e = [h_ref[slot, slab, pl.ds(HALO - 1 + k, half, stride=2), :] for k in range(4)]
        w = cw_ref[:, col:col + LANES]
        bias = cb_ref[:, col:col + LANES]
        even = e[0] * w[0:1] + e[1] * w[1:2] + e[2] * w[2:3] + bias
        odd = e[1] * w[0:1] + e[2] * w[1:2] + e[3] * w[2:3] + bias
        return even, odd

    def gate(gt, ut):
        return 0.5 * gt * (1.0 + lax.erf(gt * (2.0 ** -0.5))) * ut

    for j in range(FFN_NCHUNK):
        slot = j % 2
        cg, cu = j * fc, D_FF + j * fc
        hg = up_proj(j, cg)
        hu = up_proj(j, cu)
        for sl in range(nslab):
            h_ref[slot, sl] = hg[:, sl * LANES:(sl + 1) * LANES]
            h_ref[slot, nslab + sl] = hu[:, sl * LANES:(sl + 1) * LANES]
        for sl in range(nslab):
            g_even, g_odd = conv_even_odd(slot, sl, cg + sl * LANES)
            u_even, u_odd = conv_even_odd(slot, nslab + sl, cu + sl * LANES)
            ai_ref[slot, sl, pl.ds(0, half, stride=2), :] = gate(g_even, u_even)
            ai_ref[slot, sl, pl.ds(1, half, stride=2), :] = gate(g_odd, u_odd)
            a_ref[:, cg + sl * LANES:cg + (sl + 1) * LANES] = ai_ref[slot, sl].astype(BF16)

    for r in range(nblk):
        blk_rows = slice(r * LN_ROWS, (r + 1) * LN_ROWS)
        ff = jnp.dot(a_ref[blk_rows, :], wdn_ref[...], preferred_element_type=F32)
        y_ref[blk_rows, :] = _layer_norm(ALPHA * x1_blocks[r] + ff, g2_ref[...], b2_ref[...])


def _mix_ffn(x2d, om, osw, seq_len, wo1, wo2, g1, b1, wup, cw, cb, wdn, g2, b2):
    n = x2d.shape[0]
    tm = FFN_TM
    seq_tiles = seq_len // tm
    hb = tm // HALO
    nhb = n // HALO
    full = lambda a: pl.BlockSpec(a.shape, lambda i: (0,) * a.ndim, pipeline_mode=pl.Buffered(1))

    def tile_and_halos(width):
        return [pl.BlockSpec((tm, width), lambda i: (i, 0)),
                pl.BlockSpec((HALO, width), lambda i: (jnp.maximum(i * hb - 1, 0), 0)),
                pl.BlockSpec((HALO, width), lambda i: (jnp.minimum((i + 1) * hb, nhb - 1), 0))]

    weights = (wo1, wo2, g1, b1, wup, cw, cb, wdn, g2, b2)
    return pl.pallas_call(
        functools.partial(_mix_ffn_kernel, seq_tiles),
        grid=(n // tm,),
        in_specs=(tile_and_halos(D_MODEL) + tile_and_halos(om.shape[1]) + tile_and_halos(osw.shape[1])
                  + [full(w) for w in weights]),
        out_specs=pl.BlockSpec((tm, D_MODEL), lambda i: (i, 0)),
        out_shape=jax.ShapeDtypeStruct((n, D_MODEL), F32),
        scratch_shapes=[pltpu.VMEM((2, 2 * FFN_FC // LANES, tm + 2 * HALO, LANES), F32),
                        pltpu.VMEM((2, FFN_FC // LANES, tm, LANES), F32),
                        pltpu.VMEM((tm, D_FF), BF16)],
        compiler_params=pltpu.CompilerParams(dimension_semantics=("parallel",),
                                             vmem_limit_bytes=MIX_FFN_VMEM_LIMIT),
        name="mix_ffn",
    )(x2d, x2d, x2d, om, om, om, osw, osw, osw, *weights)


def kernel(x, positions, w_in, q_norm_g, w_q_b, kv_norm_g, w_kv_b, swa_sinks, w_o,
           ln1_g, ln1_b, w_up, conv_w, conv_b, w_down, ln2_g, ln2_b):
    b, s, d = x.shape
    n = b * s
    x2d = x.reshape(n, d)

    win_r, wq_r, wkv_r = _relayout_weights(w_in, w_q_b, w_kv_b)
    q, k, v, qs, ks4, vs2 = _projections(
        x2d, positions, win_r, q_norm_g.reshape(1, -1), wq_r, kv_norm_g.reshape(1, -1), wkv_r)

    o_mla = _mla_attention(q.reshape(b, s, -1), k.reshape(b, s, -1), v.reshape(b, s, -1))
    o_swa = _swa_attention(qs.reshape(b, s, -1), ks4.reshape(b, s, -1), vs2.reshape(b, s, -1),
                           positions, swa_sinks)

    n_mla = MLA_HEADS * V_DIM
    wo = w_o.astype(BF16)
    y = _mix_ffn(x2d, o_mla.reshape(n, -1), o_swa.reshape(n, -1), s, wo[:n_mla], wo[n_mla:],
                 ln1_g.reshape(1, -1), ln1_b.reshape(1, -1),
                 w_up.astype(BF16), conv_w.reshape(3, 2 * D_FF), conv_b.reshape(1, 2 * D_FF),
                 w_down.astype(BF16), ln2_g.reshape(1, -1), ln2_b.reshape(1, -1))
    return y.reshape(b, s, d)
```

```python
import functools

import numpy as np
import jax
import jax.numpy as jnp
from jax import lax
from jax.experimental import pallas as pl
from jax.experimental.pallas import tpu as pltpu

F32 = jnp.float32
BF16 = jnp.bfloat16

D_MODEL = 1024
MLA_HEADS = 8
Q_LORA = 256
KV_LORA = 256
NOPE = 64
ROPE = 32
HALF_ROPE = ROPE // 2
V_DIM = 64
ROPE_THETA = 10000.0
SWA_HEADS = 8
SWA_KV_HEADS = 2
SWA_GROUP = SWA_HEADS // SWA_KV_HEADS
SWA_DIM = 64
WINDOW = 128
BAND = 128
D_FF = 2816
LN_EPS = 1e-5
RMS_EPS = 1e-6
DEPTH = 1
ALPHA = (2.0 * DEPTH) ** 0.25
LOG2_E = 1.4426950408889634
MASK_DIST = 1e30

LANES = 128
SUBLANES = 8
VMEM_LIMIT = 56 * 1024 * 1024

PROJ_TM = 512
SWA_TQ = 512
SWA_VT_ROWS = SWA_DIM + 16
MLA_TQ = 512
MLA_TK = 256
MLA_SCORE_BUFS = 4
MLA_VT_ROWS = V_DIM + 16
FFN_TM = 1024
LN_ROWS = 256
HALO = 16
MIX_FFN_VMEM_LIMIT = 62 * 1024 * 1024
FFN_FC = 256
FFN_NCHUNK = D_FF // FFN_FC

_OFF_CQ = 0
_OFF_CKV = _OFF_CQ + Q_LORA
_OFF_QS = _OFF_CKV + KV_LORA
_OFF_KS = _OFF_QS + SWA_HEADS * SWA_DIM
_OFF_VS = _OFF_KS + SWA_KV_HEADS * SWA_DIM
_OFF_KR = _OFF_VS + SWA_KV_HEADS * SWA_DIM
IN_WIDTH_R = _OFF_KR + LANES


def _nt_dot(a, b):
    return lax.dot_general(a, b, (((1,), (1,)), ((), ())), preferred_element_type=F32)


def _rms(c, g):
    r = lax.rsqrt(jnp.mean(c * c, axis=-1, keepdims=True) + RMS_EPS)
    return c * r * g


def _layer_norm(y, g, b):
    mu = jnp.mean(y, axis=-1, keepdims=True)
    d = y - mu
    var = jnp.mean(d * d, axis=-1, keepdims=True)
    return d * lax.rsqrt(var + LN_EPS) * g + b


def _rope_tables(pos_row, invf_col):
    tm = pos_row.shape[1]
    ang = invf_col * pos_row
    cos_t = jnp.cos(ang)
    sin_t = jnp.sin(ang)
    pad = jnp.zeros((LANES - NOPE - ROPE, tm), F32)
    c_tab = jnp.concatenate([jnp.ones((NOPE, tm), F32), cos_t, cos_t, pad], axis=0).T
    s_tab = jnp.concatenate([jnp.zeros((NOPE, tm), F32), -sin_t, sin_t, pad], axis=0).T
    return c_tab, s_tab


def _proj_kernel(x_ref, pos_ref, invf_ref, win_ref, gq_ref, wq_ref, gkv_ref, wkv_ref,
                 q_ref, k_ref, v_ref, qs_ref, ks_ref, vs_ref):
    tm = x_ref.shape[0]
    xb = x_ref[...].astype(BF16)
    proj = jnp.dot(xb, win_ref[...], preferred_element_type=F32)
    c_tab, s_tab = _rope_tables(pos_ref[0].astype(F32), invf_ref[...])
    lane = lax.broadcasted_iota(jnp.int32, (tm, LANES), 1)
    lo = lane < SWA_DIM

    cqn = _rms(proj[:, _OFF_CQ:_OFF_CQ + Q_LORA], gq_ref[...]).astype(BF16)
    q = jnp.dot(cqn, wq_ref[...], preferred_element_type=F32)
    q_scale = (NOPE + ROPE) ** -0.5 * LOG2_E
    for h in range(MLA_HEADS):
        qh = q[:, h * LANES:(h + 1) * LANES]
        qsw = pltpu.roll(qh, LANES - ROPE, 1)
        q_ref[:, h * LANES:(h + 1) * LANES] = ((qh * c_tab + qsw * s_tab) * q_scale).astype(BF16)

    ckvn = _rms(proj[:, _OFF_CKV:_OFF_CKV + KV_LORA], gkv_ref[...]).astype(BF16)
    kv = jnp.dot(ckvn, wkv_ref[...], preferred_element_type=F32)
    kr_blk = proj[:, _OFF_KR:_OFF_KR + LANES]
    kr_sw = pltpu.roll(kr_blk, LANES // 2, 1)
    rope_lanes = (lane >= NOPE) & (lane < NOPE + ROPE)
    kr = jnp.where(rope_lanes, kr_blk * c_tab + kr_sw * s_tab, 0.0)
    for h in range(MLA_HEADS):
        k_ref[:, h * LANES:(h + 1) * LANES] = (kv[:, h * LANES:(h + 1) * LANES] + kr).astype(BF16)
    v_ref[...] = kv[:, MLA_HEADS * LANES:].astype(BF16)

    qs_scale = SWA_DIM ** -0.5 * LOG2_E
    qs_ref[...] = (proj[:, _OFF_QS:_OFF_QS + SWA_HEADS * SWA_DIM] * qs_scale).astype(BF16)
    ks = proj[:, _OFF_KS:_OFF_KS + LANES]
    ksr = pltpu.roll(ks, LANES // 2, 1)
    ks_ref[:, 0 * LANES:1 * LANES] = jnp.where(lo, ks, 0.0).astype(BF16)
    ks_ref[:, 1 * LANES:2 * LANES] = jnp.where(lo, 0.0, ksr).astype(BF16)
    ks_ref[:, 2 * LANES:3 * LANES] = jnp.where(lo, ksr, 0.0).astype(BF16)
    ks_ref[:, 3 * LANES:4 * LANES] = jnp.where(lo, 0.0, ks).astype(BF16)
    vs = proj[:, _OFF_VS:_OFF_VS + LANES]
    vsr = pltpu.roll(vs, LANES // 2, 1)
    vs_ref[:, 0 * LANES:1 * LANES] = jnp.where(lo, vs, 1.0).astype(BF16)
    vs_ref[:, 1 * LANES:2 * LANES] = jnp.where(lo, vsr, 1.0).astype(BF16)


def _relayout_weights(w_in, w_q_b, w_kv_b):
    d = w_in.shape[0]
    o = np.cumsum((Q_LORA, KV_LORA, ROPE, SWA_HEADS * SWA_DIM, SWA_KV_HEADS * SWA_DIM,
                   SWA_KV_HEADS * SWA_DIM)).tolist()
    w_cq, w_ckv, w_kr = w_in[:, :o[0]], w_in[:, o[0]:o[1]], w_in[:, o[1]:o[2]]
    w_qs, w_ks, w_vs = w_in[:, o[2]:o[3]], w_in[:, o[3]:o[4]], w_in[:, o[4]:o[5]]
    x1, x2 = w_kr[:, :HALF_ROPE], w_kr[:, HALF_ROPE:]
    z32 = jnp.zeros((d, LANES // 2 - ROPE), w_in.dtype)
    kr_blk = jnp.concatenate([x2, x1, z32, x1, x2, z32], axis=1)
    win_r = jnp.concatenate([w_cq, w_ckv, w_qs, w_ks, w_vs, kr_blk], axis=1).astype(BF16)

    wq = w_q_b.reshape(Q_LORA, MLA_HEADS, NOPE + ROPE)
    qn, q1, q2 = wq[..., :NOPE], wq[..., NOPE:NOPE + HALF_ROPE], wq[..., NOPE + HALF_ROPE:]
    wq_r = jnp.concatenate([qn, q1, q2, q2, q1], axis=-1).reshape(Q_LORA, MLA_HEADS * LANES).astype(BF16)

    wkv = w_kv_b.reshape(KV_LORA, MLA_HEADS, NOPE + V_DIM)
    kn = jnp.concatenate([wkv[..., :NOPE], jnp.zeros((KV_LORA, MLA_HEADS, LANES - NOPE), w_kv_b.dtype)], axis=-1)
    wkv_r = jnp.concatenate([kn.reshape(KV_LORA, MLA_HEADS * LANES),
                             wkv[..., NOPE:].reshape(KV_LORA, MLA_HEADS * V_DIM)], axis=1).astype(BF16)
    return win_r, wq_r, wkv_r


def _projections(x2d, positions, win_r, gq, wq_r, gkv, wkv_r):
    n = x2d.shape[0]
    tm = PROJ_TM
    inv_freq = ROPE_THETA ** (-jnp.arange(0, ROPE, 2, dtype=F32) / ROPE)
    invf_col = inv_freq.reshape(HALF_ROPE, 1)
    pos_rows = positions.reshape(n // tm, 1, tm)
    row = lambda w: pl.BlockSpec((tm, w), lambda i: (i, 0))
    full = lambda a: pl.BlockSpec(a.shape, lambda i: (0,) * a.ndim)
    widths = (MLA_HEADS * LANES, MLA_HEADS * LANES, MLA_HEADS * V_DIM,
              SWA_HEADS * SWA_DIM, 4 * LANES, 2 * LANES)
    return pl.pallas_call(
        _proj_kernel,
        grid=(n // tm,),
        in_specs=[row(D_MODEL), pl.BlockSpec((1, 1, tm), lambda i: (i, 0, 0)), full(invf_col),
                  full(win_r), full(gq), full(wq_r), full(gkv), full(wkv_r)],
        out_specs=[row(w) for w in widths],
        out_shape=[jax.ShapeDtypeStruct((n, w), BF16) for w in widths],
        compiler_params=pltpu.CompilerParams(dimension_semantics=("parallel",),
                                             vmem_limit_bytes=VMEM_LIMIT),
        name="proj",
    )(x2d, pos_rows, invf_col, win_r, gq, wq_r, gkv, wkv_r)


def _mla_body(q_ref, k_ref, v_ref, o_ref, vt_ref, st_ref, acc_ref):
    tq = q_ref.shape[1]
    tk = MLA_TK
    nc = k_ref.shape[1] // tk

    @pl.when(pl.program_id(2) == 0)
    def _():
        ones = jnp.ones((MLA_VT_ROWS - V_DIM, tk), BF16)
        for c in range(nc):
            vt = v_ref[0, c * tk:(c + 1) * tk, :].astype(F32).T.astype(BF16)
            for hh in range(2):
                vt_ref[c, hh, 0:V_DIM, :] = vt[hh * V_DIM:(hh + 1) * V_DIM]
                vt_ref[c, hh, V_DIM:MLA_VT_ROWS, :] = ones

    nbuf = st_ref.shape[0]
    ahead = nbuf - 1
    outs = []
    for hh in range(2):
        q = q_ref[0, :, hh * LANES:(hh + 1) * LANES]

        def scores(c):
            st_ref[c % nbuf] = _nt_dot(k_ref[0, c * tk:(c + 1) * tk, hh * LANES:(hh + 1) * LANES], q)

        def consume(c, m):
            st = st_ref[c % nbuf]
            m_new = jnp.maximum(m, jnp.max(st, axis=0, keepdims=True))
            p = jnp.exp2(st - m_new).astype(BF16)
            pv = jnp.dot(vt_ref[c, hh], p, preferred_element_type=F32)
            acc_ref[...] = jnp.exp2(m - m_new) * acc_ref[...] + pv
            return m_new

        acc_ref[...] = jnp.zeros_like(acc_ref)
        for c in range(ahead):
            scores(c)
        m = jnp.full((1, tq), -jnp.inf, F32)
        for c in range(nc):
            if c + ahead < nc:
                scores(c + ahead)
            m = consume(c, m)
        outs.append(acc_ref[0:V_DIM, :] / acc_ref[V_DIM:V_DIM + 1, :])
    o_ref[0] = jnp.concatenate(outs, axis=0).T.astype(BF16)


def _mla_attention(q, k, v):
    b, s, _ = q.shape
    tq = MLA_TQ
    return pl.pallas_call(
        _mla_body,
        grid=(b, MLA_HEADS // 2, s // tq),
        in_specs=[pl.BlockSpec((1, tq, 2 * LANES), lambda bi, hp, i: (bi, i, hp)),
                  pl.BlockSpec((1, s, 2 * LANES), lambda bi, hp, i: (bi, 0, hp)),
                  pl.BlockSpec((1, s, LANES), lambda bi, hp, i: (bi, 0, hp))],
        out_specs=pl.BlockSpec((1, tq, LANES), lambda bi, hp, i: (bi, i, hp)),
        out_shape=jax.ShapeDtypeStruct((b, s, MLA_HEADS * V_DIM), BF16),
        scratch_shapes=[pltpu.VMEM((s // MLA_TK, 2, MLA_VT_ROWS, MLA_TK), BF16),
                        pltpu.VMEM((MLA_SCORE_BUFS, MLA_TK, tq), F32),
                        pltpu.VMEM((MLA_VT_ROWS, tq), F32)],
        compiler_params=pltpu.CompilerParams(
            dimension_semantics=("parallel", "parallel", "arbitrary"),
            vmem_limit_bytes=VMEM_LIMIT),
        name="mla_attn",
    )(q, k, v)


def _alibi_slopes(n_heads):
    return 2.0 ** (-8.0 * (np.arange(n_heads, dtype=np.float32) + 1.0) / n_heads)


def _swa_body(first_of_batch, blocks, sink_ref, qs_ref, ks_ref, vs_ref, pcol_ref, prow_ref, o_ref, vt_ref):
    nb = prow_ref.shape[1]
    win = 3 * BAND
    ki = lax.broadcasted_iota(jnp.int32, (win, BAND), 0)
    qi = lax.broadcasted_iota(jnp.int32, (win, BAND), 1)
    slopes = _alibi_slopes(SWA_HEADS)

    @pl.when(first_of_batch)
    def _():
        for kvh in range(SWA_KV_HEADS):
            for blk in range(nb):
                vt = vs_ref[0, blk * BAND:(blk + 1) * BAND, kvh * LANES:(kvh + 1) * LANES]
                vt_ref[kvh, blk] = vt.astype(F32).T[0:SWA_VT_ROWS].astype(BF16)

    for i, r0 in blocks:
        start_blk = jnp.clip(i - 1, 0, nb - 3)
        start = pl.multiple_of(start_blk * BAND, BAND)
        pk = pcol_ref[0, pl.ds(start, win), :].astype(F32)
        pq = prow_ref[0, pl.ds(i, 1), :].astype(F32)
        rel = ki - qi + (start - i * BAND)
        dist = jnp.where(jnp.abs(rel) <= WINDOW, jnp.abs(pk - pq), MASK_DIST)
        for kvh in range(SWA_KV_HEADS):
            q_pairs = jnp.concatenate(
                [qs_ref[0, r0:r0 + BAND, (2 * kvh) * LANES:(2 * kvh + 1) * LANES],
                 qs_ref[0, r0:r0 + BAND, (2 * kvh + 1) * LANES:(2 * kvh + 2) * LANES]], axis=0)
            k_lo = ks_ref[0, pl.ds(start, win), (2 * kvh) * LANES:(2 * kvh + 1) * LANES]
            k_hi = ks_ref[0, pl.ds(start, win), (2 * kvh + 1) * LANES:(2 * kvh + 2) * LANES]
            s_par = (_nt_dot(k_lo, q_pairs), _nt_dot(k_hi, q_pairs))
            es, sink_w = [], []
            for g in range(SWA_GROUP):
                h = kvh * SWA_GROUP + g
                pair, par = g // 2, g % 2
                s = s_par[par][:, pair * BAND:(pair + 1) * BAND]
                s = s - float(slopes[h] * LOG2_E) * dist
                sk = sink_ref[h] * LOG2_E
                m = jnp.maximum(jnp.max(s, axis=0, keepdims=True), sk)
                sink_w.append(jnp.exp2(sk - m))
                es.append(jnp.exp2(s - m).astype(BF16))
            vt_win = jnp.concatenate([vt_ref[kvh, start_blk + j] for j in range(3)], axis=1)
            ot = jnp.dot(vt_win, jnp.concatenate(es, axis=1), preferred_element_type=F32)
            for pair in range(2):
                heads = []
                for par in range(2):
                    g = 2 * pair + par
                    og = ot[:, g * BAND:(g + 1) * BAND]
                    heads.append(og[0:SWA_DIM] / (og[SWA_DIM:SWA_DIM + 1] + sink_w[g]))
                col = (kvh * 2 + pair) * LANES
                o_ref[0, r0:r0 + BAND, col:col + LANES] = jnp.concatenate(heads, axis=0).T.astype(BF16)


def _swa_kernel(sink_ref, qs_ref, ks_ref, vs_ref, pcol_ref, prow_ref, o_ref, vt_ref):
    sub = qs_ref.shape[1] // BAND
    step = pl.program_id(1)
    _swa_body(step == 0, [(step * sub + jb, jb * BAND) for jb in range(sub)],
              sink_ref, qs_ref, ks_ref, vs_ref, pcol_ref, prow_ref, o_ref, vt_ref)


def _swa_attention(qs, ks4, vs2, positions, sinks):
    b, s, _ = qs.shape
    nb = s // BAND
    tq = SWA_TQ
    pcol = positions.reshape(b, s, 1)
    prow = positions.reshape(b, nb, BAND)
    return pl.pallas_call(
        _swa_kernel,
        grid=(b, s // tq),
        in_specs=[pl.BlockSpec(memory_space=pltpu.SMEM),
                  pl.BlockSpec((1, tq, SWA_HEADS * SWA_DIM), lambda bi, i: (bi, i, 0)),
                  pl.BlockSpec((1, s, 4 * LANES), lambda bi, i: (bi, 0, 0)),
                  pl.BlockSpec((1, s, 2 * LANES), lambda bi, i: (bi, 0, 0)),
                  pl.BlockSpec((1, s, 1), lambda bi, i: (bi, 0, 0)),
                  pl.BlockSpec((1, nb, BAND), lambda bi, i: (bi, 0, 0))],
        out_specs=pl.BlockSpec((1, tq, SWA_HEADS * SWA_DIM), lambda bi, i: (bi, i, 0)),
        out_shape=jax.ShapeDtypeStruct((b, s, SWA_HEADS * SWA_DIM), BF16),
        scratch_shapes=[pltpu.VMEM((SWA_KV_HEADS, nb, SWA_VT_ROWS, BAND), BF16)],
        compiler_params=pltpu.CompilerParams(dimension_semantics=("parallel", "arbitrary"),
                                             vmem_limit_bytes=VMEM_LIMIT),
        name="swa_attn",
    )(sinks.astype(F32), qs, ks4, vs2, pcol, prow)


def _mix_ffn_kernel(seq_tiles, x_ref, xp_ref, xn_ref, om_ref, omp_ref, omn_ref, os_ref, osp_ref,
                    osn_ref, wo1_ref, wo2_ref, g1_ref, b1_ref, wup_ref, cw_ref, cb_ref, wdn_ref,
                    g2_ref, b2_ref, y_ref, h_ref, ai_ref, a_ref):
    tm = x_ref.shape[0]
    fc = FFN_FC
    half = tm // 2
    nslab = fc // LANES
    rows = tm + 2 * HALO
    i = pl.program_id(0)
    t = i % seq_tiles

    nblk = tm // LN_ROWS

    def ext(r, main_ref, prev_ref, next_ref):
        parts = [main_ref[r * LN_ROWS:(r + 1) * LN_ROWS, :]]
        if r == 0:
            parts.insert(0, prev_ref[...])
        if r == nblk - 1:
            parts.append(next_ref[...])
        return parts[0] if len(parts) == 1 else jnp.concatenate(parts, axis=0)

    x1e_blocks = []
    for r in range(nblk):
        mix = jnp.dot(ext(r, om_ref, omp_ref, omn_ref), wo1_ref[...], preferred_element_type=F32)
        mix = mix + jnp.dot(ext(r, os_ref, osp_ref, osn_ref), wo2_ref[...], preferred_element_type=F32)
        blk = _layer_norm(ALPHA * ext(r, x_ref, xp_ref, xn_ref) + mix, g1_ref[...], b1_ref[...])
        row = lax.broadcasted_iota(jnp.int32, (blk.shape[0], 1), 0)
        if r == 0:
            blk = jnp.where((t == 0) & (row < HALO), 0.0, blk)
        if r == nblk - 1:
            first_next = blk.shape[0] - HALO
            blk = jnp.where((t == seq_tiles - 1) & (row >= first_next), 0.0, blk)
        x1e_blocks.append(blk)
    xe_blocks = [blk.astype(BF16) for blk in x1e_blocks]
    xe = jnp.concatenate(xe_blocks, axis=0)
    x1_blocks = list(x1e_blocks)
    x1_blocks[0] = x1_blocks[0][HALO:]
    x1_blocks[-1] = x1_blocks[-1][:x1_blocks[-1].shape[0] - HALO]

    def up_proj(j, cols):
        w = wup_ref[:, cols:cols + fc]
        if j == 0:
            return jnp.concatenate([jnp.dot(xb, w, preferred_element_type=F32) for xb in xe_blocks], axis=0)
        return jnp.dot(xe, w, preferred_element_type=F32)

    def conv_even_odd(slot, slab, col):
        e = [h_ref[slot, slab, pl.ds(HALO - 1 + k, half, stride=2), :] for k in range(4)]
        w = cw_ref[:, col:col + LANES]
        bias = cb_ref[:, col:col + LANES]
        even = e[0] * w[0:1] + e[1] * w[1:2] + e[2] * w[2:3] + bias
        odd = e[1] * w[0:1] + e[2] * w[1:2] + e[3] * w[2:3] + bias
        return even, odd

    def gate(gt, ut):
        return 0.5 * gt * (1.0 + lax.erf(gt * (2.0 ** -0.5))) * ut

    for j in range(FFN_NCHUNK):
        slot = j % 2
        cg, cu = j * fc, D_FF + j * fc
        hg = up_proj(j, cg)
        hu = up_proj(j, cu)
        for sl in range(nslab):
            h_ref[slot, sl] = hg[:, sl * LANES:(sl + 1) * LANES]
            h_ref[slot, nslab + sl] = hu[:, sl * LANES:(sl + 1) * LANES]
        for sl in range(nslab):
            g_even, g_odd = conv_even_odd(slot, sl, cg + sl * LANES)
            u_even, u_odd = conv_even_odd(slot, nslab + sl, cu + sl * LANES)
            ai_ref[slot, sl, pl.ds(0, half, stride=2), :] = gate(g_even, u_even)
            ai_ref[slot, sl, pl.ds(1, half, stride=2), :] = gate(g_odd, u_odd)
            a_ref[:, cg + sl * LANES:cg + (sl + 1) * LANES] = ai_ref[slot, sl].astype(BF16)

    for r in range(nblk):
        blk_rows = slice(r * LN_ROWS, (r + 1) * LN_ROWS)
        ff = jnp.dot(a_ref[blk_rows, :], wdn_ref[...], preferred_element_type=F32)
        y_ref[blk_rows, :] = _layer_norm(ALPHA * x1_blocks[r] + ff, g2_ref[...], b2_ref[...])


def _mix_ffn(x2d, om, osw, seq_len, wo1, wo2, g1, b1, wup, cw, cb, wdn, g2, b2):
    n = x2d.shape[0]
    tm = FFN_TM
    seq_tiles = seq_len // tm
    hb = tm // HALO
    nhb = n // HALO
    full = lambda a: pl.BlockSpec(a.shape, lambda i: (0,) * a.ndim, pipeline_mode=pl.Buffered(1))

    def tile_and_halos(width):
        return [pl.BlockSpec((tm, width), lambda i: (i, 0)),
                pl.BlockSpec((HALO, width), lambda i: (jnp.maximum(i * hb - 1, 0), 0)),
                pl.BlockSpec((HALO, width), lambda i: (jnp.minimum((i + 1) * hb, nhb - 1), 0))]

    weights = (wo1, wo2, g1, b1, wup, cw, cb, wdn, g2, b2)
    return pl.pallas_call(
        functools.partial(_mix_ffn_kernel, seq_tiles),
        grid=(n // tm,),
        in_specs=(tile_and_halos(D_MODEL) + tile_and_halos(om.shape[1]) + tile_and_halos(osw.shape[1])
                  + [full(w) for w in weights]),
        out_specs=pl.BlockSpec((tm, D_MODEL), lambda i: (i, 0)),
        out_shape=jax.ShapeDtypeStruct((n, D_MODEL), F32),
        scratch_shapes=[pltpu.VMEM((2, 2 * FFN_FC // LANES, tm + 2 * HALO, LANES), F32),
                        pltpu.VMEM((2, FFN_FC // LANES, tm, LANES), F32),
                        pltpu.VMEM((tm, D_FF), BF16)],
        compiler_params=pltpu.CompilerParams(dimension_semantics=("parallel",),
                                             vmem_limit_bytes=MIX_FFN_VMEM_LIMIT),
        name="mix_ffn",
    )(x2d, x2d, x2d, om, om, om, osw, osw, osw, *weights)


def kernel(x, positions, w_in, q_norm_g, w_q_b, kv_norm_g, w_kv_b, swa_sinks, w_o,
           ln1_g, ln1_b, w_up, conv_w, conv_b, w_down, ln2_g, ln2_b):
    b, s, d = x.shape
    n = b * s
    x2d = x.reshape(n, d)

    win_r, wq_r, wkv_r = _relayout_weights(w_in, w_q_b, w_kv_b)
    q, k, v, qs, ks4, vs2 = _projections(
        x2d, positions, win_r, q_norm_g.reshape(1, -1), wq_r, kv_norm_g.reshape(1, -1), wkv_r)

    o_mla = _mla_attention(q.reshape(b, s, -1), k.reshape(b, s, -1), v.reshape(b, s, -1))
    o_swa = _swa_attention(qs.reshape(b, s, -1), ks4.reshape(b, s, -1), vs2.reshape(b, s, -1),
                           positions, swa_sinks)

    n_mla = MLA_HEADS * V_DIM
    wo = w_o.astype(BF16)
    y = _mix_ffn(x2d, o_mla.reshape(n, -1), o_swa.reshape(n, -1), s, wo[:n_mla], wo[n_mla:],
                 ln1_g.reshape(1, -1), ln1_b.reshape(1, -1),
                 w_up.astype(BF16), conv_w.reshape(3, 2 * D_FF), conv_b.reshape(1, 2 * D_FF),
                 w_down.astype(BF16), ln2_g.reshape(1, -1), ln2_b.reshape(1, -1))
    return y.reshape(b, s, d)
```

```python
import functools

import numpy as np
import jax
import jax.numpy as jnp
from jax import lax
from jax.experimental import pallas as pl
from jax.experimental.pallas import tpu as pltpu

F32 = jnp.float32
BF16 = jnp.bfloat16

D_MODEL = 1024
MLA_HEADS = 8
Q_LORA = 256
KV_LORA = 256
NOPE = 64
ROPE = 32
HALF_ROPE = ROPE // 2
V_DIM = 64
ROPE_THETA = 10000.0
SWA_HEADS = 8
SWA_KV_HEADS = 2
SWA_GROUP = SWA_HEADS // SWA_KV_HEADS
SWA_DIM = 64
WINDOW = 128
BAND = 128
D_FF = 2816
LN_EPS = 1e-5
RMS_EPS = 1e-6
DEPTH = 1
ALPHA = (2.0 * DEPTH) ** 0.25
LOG2_E = 1.4426950408889634
MASK_DIST = 1e30

LANES = 128
SUBLANES = 8
VMEM_LIMIT = 56 * 1024 * 1024

PROJ_TM = 512
SWA_TQ = 512
SWA_SCORE_BUFS = 4
SWA_VT_ROWS = SWA_DIM + 16
MLA_TQ = 512
MLA_TK = 256
MLA_SCORE_BUFS = 4
MLA_VT_ROWS = V_DIM + 16
FFN_TM = 1024
LN_ROWS = 256
HALO = 16
MIX_FFN_VMEM_LIMIT = 62 * 1024 * 1024
FFN_FC = 256
FFN_NCHUNK = D_FF // FFN_FC

_OFF_CQ = 0
_OFF_CKV = _OFF_CQ + Q_LORA
_OFF_QS = _OFF_CKV + KV_LORA
_OFF_KS = _OFF_QS + SWA_HEADS * SWA_DIM
_OFF_VS = _OFF_KS + SWA_KV_HEADS * SWA_DIM
_OFF_KR = _OFF_VS + SWA_KV_HEADS * SWA_DIM
IN_WIDTH_R = _OFF_KR + LANES


def _nt_dot(a, b):
    return lax.dot_general(a, b, (((1,), (1,)), ((), ())), preferred_element_type=F32)


def _rms(c, g):
    r = lax.rsqrt(jnp.mean(c * c, axis=-1, keepdims=True) + RMS_EPS)
    return c * r * g


def _layer_norm(y, g, b):
    mu = jnp.mean(y, axis=-1, keepdims=True)
    d = y - mu
    var = jnp.mean(d * d, axis=-1, keepdims=True)
    return d * lax.rsqrt(var + LN_EPS) * g + b


def _rope_tables(pos_row, invf_col):
    tm = pos_row.shape[1]
    ang = invf_col * pos_row
    cos_t = jnp.cos(ang)
    sin_t = jnp.sin(ang)
    pad = jnp.zeros((LANES - NOPE - ROPE, tm), F32)
    c_tab = jnp.concatenate([jnp.ones((NOPE, tm), F32), cos_t, cos_t, pad], axis=0).T
    s_tab = jnp.concatenate([jnp.zeros((NOPE, tm), F32), -sin_t, sin_t, pad], axis=0).T
    return c_tab, s_tab


def _proj_kernel(x_ref, pos_ref, invf_ref, win_ref, gq_ref, wq_ref, gkv_ref, wkv_ref,
                 q_ref, k_ref, v_ref, qs_ref, ks_ref, vs_ref):
    tm = x_ref.shape[0]
    xb = x_ref[...].astype(BF16)
    proj = jnp.dot(xb, win_ref[...], preferred_element_type=F32)
    c_tab, s_tab = _rope_tables(pos_ref[0].astype(F32), invf_ref[...])
    lane = lax.broadcasted_iota(jnp.int32, (tm, LANES), 1)
    lo = lane < SWA_DIM

    cqn = _rms(proj[:, _OFF_CQ:_OFF_CQ + Q_LORA], gq_ref[...]).astype(BF16)
    q = jnp.dot(cqn, wq_ref[...], preferred_element_type=F32)
    q_scale = (NOPE + ROPE) ** -0.5 * LOG2_E
    for h in range(MLA_HEADS):
        qh = q[:, h * LANES:(h + 1) * LANES]
        qsw = pltpu.roll(qh, LANES - ROPE, 1)
        q_ref[:, h * LANES:(h + 1) * LANES] = ((qh * c_tab + qsw * s_tab) * q_scale).astype(BF16)

    ckvn = _rms(proj[:, _OFF_CKV:_OFF_CKV + KV_LORA], gkv_ref[...]).astype(BF16)
    kv = jnp.dot(ckvn, wkv_ref[...], preferred_element_type=F32)
    kr_blk = proj[:, _OFF_KR:_OFF_KR + LANES]
    kr_sw = pltpu.roll(kr_blk, LANES // 2, 1)
    rope_lanes = (lane >= NOPE) & (lane < NOPE + ROPE)
    kr = jnp.where(rope_lanes, kr_blk * c_tab + kr_sw * s_tab, 0.0)
    for h in range(MLA_HEADS):
        k_ref[:, h * LANES:(h + 1) * LANES] = (kv[:, h * LANES:(h + 1) * LANES] + kr).astype(BF16)
    v_ref[...] = kv[:, MLA_HEADS * LANES:].astype(BF16)

    qs_scale = SWA_DIM ** -0.5 * LOG2_E
    qs_ref[...] = (proj[:, _OFF_QS:_OFF_QS + SWA_HEADS * SWA_DIM] * qs_scale).astype(BF16)
    ks = proj[:, _OFF_KS:_OFF_KS + LANES]
    ksr = pltpu.roll(ks, LANES // 2, 1)
    ks_ref[:, 0 * LANES:1 * LANES] = jnp.where(lo, ks, 0.0).astype(BF16)
    ks_ref[:, 1 * LANES:2 * LANES] = jnp.where(lo, 0.0, ksr).astype(BF16)
    ks_ref[:, 2 * LANES:3 * LANES] = jnp.where(lo, ksr, 0.0).astype(BF16)
    ks_ref[:, 3 * LANES:4 * LANES] = jnp.where(lo, 0.0, ks).astype(BF16)
    vs = proj[:, _OFF_VS:_OFF_VS + LANES]
    vsr = pltpu.roll(vs, LANES // 2, 1)
    vs_ref[:, 0 * LANES:1 * LANES] = jnp.where(lo, vs, 1.0).astype(BF16)
    vs_ref[:, 1 * LANES:2 * LANES] = jnp.where(lo, vsr, 1.0).astype(BF16)


def _relayout_weights(w_in, w_q_b, w_kv_b):
    d = w_in.shape[0]
    o = np.cumsum((Q_LORA, KV_LORA, ROPE, SWA_HEADS * SWA_DIM, SWA_KV_HEADS * SWA_DIM,
                   SWA_KV_HEADS * SWA_DIM)).tolist()
    w_cq, w_ckv, w_kr = w_in[:, :o[0]], w_in[:, o[0]:o[1]], w_in[:, o[1]:o[2]]
    w_qs, w_ks, w_vs = w_in[:, o[2]:o[3]], w_in[:, o[3]:o[4]], w_in[:, o[4]:o[5]]
    x1, x2 = w_kr[:, :HALF_ROPE], w_kr[:, HALF_ROPE:]
    z32 = jnp.zeros((d, LANES // 2 - ROPE), w_in.dtype)
    kr_blk = jnp.concatenate([x2, x1, z32, x1, x2, z32], axis=1)
    win_r = jnp.concatenate([w_cq, w_ckv, w_qs, w_ks, w_vs, kr_blk], axis=1).astype(BF16)

    wq = w_q_b.reshape(Q_LORA, MLA_HEADS, NOPE + ROPE)
    qn, q1, q2 = wq[..., :NOPE], wq[..., NOPE:NOPE + HALF_ROPE], wq[..., NOPE + HALF_ROPE:]
    wq_r = jnp.concatenate([qn, q1, q2, q2, q1], axis=-1).reshape(Q_LORA, MLA_HEADS * LANES).astype(BF16)

    wkv = w_kv_b.reshape(KV_LORA, MLA_HEADS, NOPE + V_DIM)
    kn = jnp.concatenate([wkv[..., :NOPE], jnp.zeros((KV_LORA, MLA_HEADS, LANES - NOPE), w_kv_b.dtype)], axis=-1)
    wkv_r = jnp.concatenate([kn.reshape(KV_LORA, MLA_HEADS * LANES),
                             wkv[..., NOPE:].reshape(KV_LORA, MLA_HEADS * V_DIM)], axis=1).astype(BF16)
    return win_r, wq_r, wkv_r


def _projections(x2d, positions, win_r, gq, wq_r, gkv, wkv_r):
    n = x2d.shape[0]
    tm = PROJ_TM
    inv_freq = ROPE_THETA ** (-jnp.arange(0, ROPE, 2, dtype=F32) / ROPE)
    invf_col = inv_freq.reshape(HALF_ROPE, 1)
    pos_rows = positions.reshape(n // tm, 1, tm)
    row = lambda w: pl.BlockSpec((tm, w), lambda i: (i, 0))
    full = lambda a: pl.BlockSpec(a.shape, lambda i: (0,) * a.ndim)
    widths = (MLA_HEADS * LANES, MLA_HEADS * LANES, MLA_HEADS * V_DIM,
              SWA_HEADS * SWA_DIM, 4 * LANES, 2 * LANES)
    return pl.pallas_call(
        _proj_kernel,
        grid=(n // tm,),
        in_specs=[row(D_MODEL), pl.BlockSpec((1, 1, tm), lambda i: (i, 0, 0)), full(invf_col),
                  full(win_r), full(gq), full(wq_r), full(gkv), full(wkv_r)],
        out_specs=[row(w) for w in widths],
        out_shape=[jax.ShapeDtypeStruct((n, w), BF16) for w in widths],
        compiler_params=pltpu.CompilerParams(dimension_semantics=("parallel",),
                                             vmem_limit_bytes=VMEM_LIMIT),
        name="proj",
    )(x2d, pos_rows, invf_col, win_r, gq, wq_r, gkv, wkv_r)


def _mla_body(q_ref, k_ref, v_ref, o_ref, vt_ref, st_ref, acc_ref):
    tq = q_ref.shape[1]
    tk = MLA_TK
    nc = k_ref.shape[1] // tk

    @pl.when(pl.program_id(2) == 0)
    def _():
        ones = jnp.ones((MLA_VT_ROWS - V_DIM, tk), BF16)
        for c in range(nc):
            vt = v_ref[0, c * tk:(c + 1) * tk, :].astype(F32).T.astype(BF16)
            for hh in range(2):
                vt_ref[c, hh, 0:V_DIM, :] = vt[hh * V_DIM:(hh + 1) * V_DIM]
                vt_ref[c, hh, V_DIM:MLA_VT_ROWS, :] = ones

    nbuf = st_ref.shape[0]
    ahead = nbuf - 1
    outs = []
    for hh in range(2):
        q = q_ref[0, :, hh * LANES:(hh + 1) * LANES]

        def scores(c):
            st_ref[c % nbuf] = _nt_dot(k_ref[0, c * tk:(c + 1) * tk, hh * LANES:(hh + 1) * LANES], q)

        def consume(c, m):
            st = st_ref[c % nbuf]
            m_new = jnp.maximum(m, jnp.max(st, axis=0, keepdims=True))
            p = jnp.exp2(st - m_new).astype(BF16)
            pv = jnp.dot(vt_ref[c, hh], p, preferred_element_type=F32)
            acc_ref[...] = jnp.exp2(m - m_new) * acc_ref[...] + pv
            return m_new

        acc_ref[...] = jnp.zeros_like(acc_ref)
        for c in range(ahead):
            scores(c)
        m = jnp.full((1, tq), -jnp.inf, F32)
        for c in range(nc):
            if c + ahead < nc:
                scores(c + ahead)
            m = consume(c, m)
        outs.append(acc_ref[0:V_DIM, :] / acc_ref[V_DIM:V_DIM + 1, :])
    o_ref[0] = jnp.concatenate(outs, axis=0).T.astype(BF16)


def _mla_attention(q, k, v):
    b, s, _ = q.shape
    tq = MLA_TQ
    return pl.pallas_call(
        _mla_body,
        grid=(b, MLA_HEADS // 2, s // tq),
        in_specs=[pl.BlockSpec((1, tq, 2 * LANES), lambda bi, hp, i: (bi, i, hp)),
                  pl.BlockSpec((1, s, 2 * LANES), lambda bi, hp, i: (bi, 0, hp)),
                  pl.BlockSpec((1, s, LANES), lambda bi, hp, i: (bi, 0, hp))],
        out_specs=pl.BlockSpec((1, tq, LANES), lambda bi, hp, i: (bi, i, hp)),
        out_shape=jax.ShapeDtypeStruct((b, s, MLA_HEADS * V_DIM), BF16),
        scratch_shapes=[pltpu.VMEM((s // MLA_TK, 2, MLA_VT_ROWS, MLA_TK), BF16),
                        pltpu.VMEM((MLA_SCORE_BUFS, MLA_TK, tq), F32),
                        pltpu.VMEM((MLA_VT_ROWS, tq), F32)],
        compiler_params=pltpu.CompilerParams(
            dimension_semantics=("parallel", "parallel", "arbitrary"),
            vmem_limit_bytes=VMEM_LIMIT),
        name="mla_attn",
    )(q, k, v)


def _alibi_slopes(n_heads):
    return 2.0 ** (-8.0 * (np.arange(n_heads, dtype=np.float32) + 1.0) / n_heads)


def _swa_body(first_of_batch, blocks, sink_ref, qs_ref, ks_ref, vs_ref, pcol_ref, prow_ref, o_ref,
              vt_ref, s_ref):
    nb = prow_ref.shape[1]
    win = 3 * BAND
    ki = lax.broadcasted_iota(jnp.int32, (win, BAND), 0)
    qi = lax.broadcasted_iota(jnp.int32, (win, BAND), 1)
    slopes = _alibi_slopes(SWA_HEADS)

    @pl.when(first_of_batch)
    def _():
        for kvh in range(SWA_KV_HEADS):
            for blk in range(nb):
                vt = vs_ref[0, blk * BAND:(blk + 1) * BAND, kvh * LANES:(kvh + 1) * LANES]
                vt_ref[kvh, blk] = vt.astype(F32).T[0:SWA_VT_ROWS].astype(BF16)

    units = [(i, r0, kvh) for i, r0 in blocks for kvh in range(SWA_KV_HEADS)]
    nbuf = s_ref.shape[0]
    ahead = nbuf - 1

    def window(i):
        start_blk = jnp.clip(i - 1, 0, nb - 3)
        return start_blk, pl.multiple_of(start_blk * BAND, BAND)

    def scores(u):
        i, r0, kvh = units[u]
        _, start = window(i)
        q_pairs = jnp.concatenate(
            [qs_ref[0, r0:r0 + BAND, (2 * kvh) * LANES:(2 * kvh + 1) * LANES],
             qs_ref[0, r0:r0 + BAND, (2 * kvh + 1) * LANES:(2 * kvh + 2) * LANES]], axis=0)
        k_lo = ks_ref[0, pl.ds(start, win), (2 * kvh) * LANES:(2 * kvh + 1) * LANES]
        k_hi = ks_ref[0, pl.ds(start, win), (2 * kvh + 1) * LANES:(2 * kvh + 2) * LANES]
        s_ref[u % nbuf, 0] = _nt_dot(k_lo, q_pairs)
        s_ref[u % nbuf, 1] = _nt_dot(k_hi, q_pairs)

    def consume(u):
        i, r0, kvh = units[u]
        start_blk, start = window(i)
        pk = pcol_ref[0, pl.ds(start, win), :].astype(F32)
        pq = prow_ref[0, pl.ds(i, 1), :].astype(F32)
        rel = ki - qi + (start - i * BAND)
        dist = jnp.where(jnp.abs(rel) <= WINDOW, jnp.abs(pk - pq), MASK_DIST)
        es, sink_w = [], []
        for g in range(SWA_GROUP):
            h = kvh * SWA_GROUP + g
            pair, par = g // 2, g % 2
            s = s_ref[u % nbuf, par, :, pair * BAND:(pair + 1) * BAND]
            s = s - float(slopes[h] * LOG2_E) * dist
            sk = sink_ref[h] * LOG2_E
            m = jnp.maximum(jnp.max(s, axis=0, keepdims=True), sk)
            sink_w.append(jnp.exp2(sk - m))
            es.append(jnp.exp2(s - m).astype(BF16))
        vt_win = jnp.concatenate([vt_ref[kvh, start_blk + j] for j in range(3)], axis=1)
        ot = jnp.dot(vt_win, jnp.concatenate(es, axis=1), preferred_element_type=F32)
        for pair in range(2):
            heads = []
            for par in range(2):
                g = 2 * pair + par
                og = ot[:, g * BAND:(g + 1) * BAND]
                heads.append(og[0:SWA_DIM] / (og[SWA_DIM:SWA_DIM + 1] + sink_w[g]))
            col = (kvh * 2 + pair) * LANES
            o_ref[0, r0:r0 + BAND, col:col + LANES] = jnp.concatenate(heads, axis=0).T.astype(BF16)

    for u in range(min(ahead, len(units))):
        scores(u)
    for u in range(len(units)):
        if u + ahead < len(units):
            scores(u + ahead)
        consume(u)


def _swa_kernel(sink_ref, qs_ref, ks_ref, vs_ref, pcol_ref, prow_ref, o_ref, vt_ref, s_ref):
    sub = qs_ref.shape[1] // BAND
    step = pl.program_id(1)
    _swa_body(step == 0, [(step * sub + jb, jb * BAND) for jb in range(sub)],
              sink_ref, qs_ref, ks_ref, vs_ref, pcol_ref, prow_ref, o_ref, vt_ref, s_ref)


def _swa_attention(qs, ks4, vs2, positions, sinks):
    b, s, _ = qs.shape
    nb = s // BAND
    tq = SWA_TQ
    pcol = positions.reshape(b, s, 1)
    prow = positions.reshape(b, nb, BAND)
    return pl.pallas_call(
        _swa_kernel,
        grid=(b, s // tq),
        in_specs=[pl.BlockSpec(memory_space=pltpu.SMEM),
                  pl.BlockSpec((1, tq, SWA_HEADS * SWA_DIM), lambda bi, i: (bi, i, 0)),
                  pl.BlockSpec((1, s, 4 * LANES), lambda bi, i: (bi, 0, 0)),
                  pl.BlockSpec((1, s, 2 * LANES), lambda bi, i: (bi, 0, 0)),
                  pl.BlockSpec((1, s, 1), lambda bi, i: (bi, 0, 0)),
                  pl.BlockSpec((1, nb, BAND), lambda bi, i: (bi, 0, 0))],
        out_specs=pl.BlockSpec((1, tq, SWA_HEADS * SWA_DIM), lambda bi, i: (bi, i, 0)),
        out_shape=jax.ShapeDtypeStruct((b, s, SWA_HEADS * SWA_DIM), BF16),
        scratch_shapes=[pltpu.VMEM((SWA_KV_HEADS, nb, SWA_VT_ROWS, BAND), BF16),
                        pltpu.VMEM((SWA_SCORE_BUFS, 2, 3 * BAND, 2 * LANES), F32)],
        compiler_params=pltpu.CompilerParams(dimension_semantics=("parallel", "arbitrary"),
                                             vmem_limit_bytes=VMEM_LIMIT),
        name="swa_attn",
    )(sinks.astype(F32), qs, ks4, vs2, pcol, prow)


def _mix_ffn_kernel(seq_tiles, x_ref, xp_ref, xn_ref, om_ref, omp_ref, omn_ref, os_ref, osp_ref,
                    osn_ref, wo1_ref, wo2_ref, g1_ref, b1_ref, wup_ref, cw_ref, cb_ref, wdn_ref,
                    g2_ref, b2_ref, y_ref, h_ref, ai_ref, a_ref):
    tm = x_ref.shape[0]
    fc = FFN_FC
    half = tm // 2
    nslab = fc // LANES
    rows = tm + 2 * HALO
    i = pl.program_id(0)
    t = i % seq_tiles

    nblk = tm // LN_ROWS

    def ext(r, main_ref, prev_ref, next_ref):
        parts = [main_ref[r * LN_ROWS:(r + 1) * LN_ROWS, :]]
        if r == 0:
            parts.insert(0, prev_ref[...])
        if r == nblk - 1:
            parts.append(next_ref[...])
        return parts[0] if len(parts) == 1 else jnp.concatenate(parts, axis=0)

    x1e_blocks = []
    for r in range(nblk):
        mix = jnp.dot(ext(r, om_ref, omp_ref, omn_ref), wo1_ref[...], preferred_element_type=F32)
        mix = mix + jnp.dot(ext(r, os_ref, osp_ref, osn_ref), wo2_ref[...], preferred_element_type=F32)
        blk = _layer_norm(ALPHA * ext(r, x_ref, xp_ref, xn_ref) + mix, g1_ref[...], b1_ref[...])
        row = lax.broadcasted_iota(jnp.int32, (blk.shape[0], 1), 0)
        if r == 0:
            blk = jnp.where((t == 0) & (row < HALO), 0.0, blk)
        if r == nblk - 1:
            first_next = blk.shape[0] - HALO
            blk = jnp.where((t == seq_tiles - 1) & (row >= first_next), 0.0, blk)
        x1e_blocks.append(blk)
    xe_blocks = [blk.astype(BF16) for blk in x1e_blocks]
    xe = jnp.concatenate(xe_blocks, axis=0)
    x1_blocks = list(x1e_blocks)
    x1_blocks[0] = x1_blocks[0][HALO:]
    x1_blocks[-1] = x1_blocks[-1][:x1_blocks[-1].shape[0] - HALO]

    def up_proj(j, cols):
        w = wup_ref[:, cols:cols + fc]
        if j == 0:
            return jnp.concatenate([jnp.dot(xb, w, preferred_element_type=F32) for xb in xe_blocks], axis=0)
        return jnp.dot(xe, w, preferred_element_type=F32)

    def conv_even_odd(slot, slab, col):
        e = [h_ref[slot, slab, pl.ds(HALO - 1 + k, half, stride=2), :] for k in range(4)]
        w = cw_ref[:, col:col + LANES]
        bias = cb_ref[:, col:col + LANES]
        even = e[0] * w[0:1] + e[1] * w[1:2] + e[2] * w[2:3] + bias
        odd = e[1] * w[0:1] + e[2] * w[1:2] + e[3] * w[2:3] + bias
        return even, odd

    def gate(gt, ut):
        return 0.5 * gt * (1.0 + lax.erf(gt * (2.0 ** -0.5))) * ut

    for j in range(FFN_NCHUNK):
        slot = j % 2
        cg, cu = j * fc, D_FF + j * fc
        hg = up_proj(j, cg)
        hu = up_proj(j, cu)
        for sl in range(nslab):
            h_ref[slot, sl] = hg[:, sl * LANES:(sl + 1) * LANES]
            h_ref[slot, nslab + sl] = hu[:, sl * LANES:(sl + 1) * LANES]
        for sl in range(nslab):
            g_even, g_odd = conv_even_odd(slot, sl, cg + sl * LANES)
            u_even, u_odd = conv_even_odd(slot, nslab + sl, cu + sl * LANES)
            ai_ref[slot, sl, pl.ds(0, half, stride=2), :] = gate(g_even, u_even)
            ai_ref[slot, sl, pl.ds(1, half, stride=2), :] = gate(g_odd, u_odd)
            a_ref[:, cg + sl * LANES:cg + (sl + 1) * LANES] = ai_ref[slot, sl].astype(BF16)

    for r in range(nblk):
        blk_rows = slice(r * LN_ROWS, (r + 1) * LN_ROWS)
        ff = jnp.dot(a_ref[blk_rows, :], wdn_ref[...], preferred_element_type=F32)
        y_ref[blk_rows, :] = _layer_norm(ALPHA * x1_blocks[r] + ff, g2_ref[...], b2_ref[...])


def _mix_ffn(x2d, om, osw, seq_len, wo1, wo2, g1, b1, wup, cw, cb, wdn, g2, b2):
    n = x2d.shape[0]
    tm = FFN_TM
    seq_tiles = seq_len // tm
    hb = tm // HALO
    nhb = n // HALO
    full = lambda a: pl.BlockSpec(a.shape, lambda i: (0,) * a.ndim, pipeline_mode=pl.Buffered(1))

    def tile_and_halos(width):
        return [pl.BlockSpec((tm, width), lambda i: (i, 0)),
                pl.BlockSpec((HALO, width), lambda i: (jnp.maximum(i * hb - 1, 0), 0)),
                pl.BlockSpec((HALO, width), lambda i: (jnp.minimum((i + 1) * hb, nhb - 1), 0))]

    weights = (wo1, wo2, g1, b1, wup, cw, cb, wdn, g2, b2)
    return pl.pallas_call(
        functools.partial(_mix_ffn_kernel, seq_tiles),
        grid=(n // tm,),
        in_specs=(tile_and_halos(D_MODEL) + tile_and_halos(om.shape[1]) + tile_and_halos(osw.shape[1])
                  + [full(w) for w in weights]),
        out_specs=pl.BlockSpec((tm, D_MODEL), lambda i: (i, 0)),
        out_shape=jax.ShapeDtypeStruct((n, D_MODEL), F32),
        scratch_shapes=[pltpu.VMEM((2, 2 * FFN_FC // LANES, tm + 2 * HALO, LANES), F32),
                        pltpu.VMEM((2, FFN_FC // LANES, tm, LANES), F32),
                        pltpu.VMEM((tm, D_FF), BF16)],
        compiler_params=pltpu.CompilerParams(dimension_semantics=("parallel",),
                                             vmem_limit_bytes=MIX_FFN_VMEM_LIMIT),
        name="mix_ffn",
    )(x2d, x2d, x2d, om, om, om, osw, osw, osw, *weights)


def kernel(x, positions, w_in, q_norm_g, w_q_b, kv_norm_g, w_kv_b, swa_sinks, w_o,
           ln1_g, ln1_b, w_up, conv_w, conv_b, w_down, ln2_g, ln2_b):
    b, s, d = x.shape
    n = b * s
    x2d = x.reshape(n, d)

    win_r, wq_r, wkv_r = _relayout_weights(w_in, w_q_b, w_kv_b)
    q, k, v, qs, ks4, vs2 = _projections(
        x2d, positions, win_r, q_norm_g.reshape(1, -1), wq_r, kv_norm_g.reshape(1, -1), wkv_r)

    o_mla = _mla_attention(q.reshape(b, s, -1), k.reshape(b, s, -1), v.reshape(b, s, -1))
    o_swa = _swa_attention(qs.reshape(b, s, -1), ks4.reshape(b, s, -1), vs2.reshape(b, s, -1),
                           positions, swa_sinks)

    n_mla = MLA_HEADS * V_DIM
    wo = w_o.astype(BF16)
    y = _mix_ffn(x2d, o_mla.reshape(n, -1), o_swa.reshape(n, -1), s, wo[:n_mla], wo[n_mla:],
                 ln1_g.reshape(1, -1), ln1_b.reshape(1, -1),
                 w_up.astype(BF16), conv_w.reshape(3, 2 * D_FF), conv_b.reshape(1, 2 * D_FF),
                 w_down.astype(BF16), ln2_g.reshape(1, -1), ln2_b.reshape(1, -1))
    return y.reshape(b, s, d)
```

```python
import functools

import numpy as np
import jax
import jax.numpy as jnp
from jax import lax
from jax.experimental import pallas as pl
from jax.experimental.pallas import tpu as pltpu

F32 = jnp.float32
BF16 = jnp.bfloat16

D_MODEL = 1024
MLA_HEADS = 8
Q_LORA = 256
KV_LORA = 256
NOPE = 64
ROPE = 32
HALF_ROPE = ROPE // 2
V_DIM = 64
ROPE_THETA = 10000.0
SWA_HEADS = 8
SWA_KV_HEADS = 2
SWA_GROUP = SWA_HEADS // SWA_KV_HEADS
SWA_DIM = 64
WINDOW = 128
BAND = 128
D_FF = 2816
LN_EPS = 1e-5
RMS_EPS = 1e-6
DEPTH = 1
ALPHA = (2.0 * DEPTH) ** 0.25
LOG2_E = 1.4426950408889634
MASK_DIST = 1e30

LANES = 128
SUBLANES = 8
VMEM_LIMIT = 56 * 1024 * 1024

PROJ_TM = 1024
SWA_TQ = 512
SWA_SCORE_BUFS = 4
SWA_VT_ROWS = SWA_DIM + 16
MLA_TQ = 512
MLA_TK = 256
MLA_SCORE_BUFS = 4
MLA_VT_ROWS = V_DIM + 16
FFN_TM = 1024
LN_ROWS = 256
HALO = 16
MIX_FFN_VMEM_LIMIT = 62 * 1024 * 1024
FFN_FC = 256
FFN_NCHUNK = D_FF // FFN_FC

_OFF_CQ = 0
_OFF_CKV = _OFF_CQ + Q_LORA
_OFF_QS = _OFF_CKV + KV_LORA
_OFF_KS = _OFF_QS + SWA_HEADS * SWA_DIM
_OFF_VS = _OFF_KS + SWA_KV_HEADS * SWA_DIM
_OFF_KR = _OFF_VS + SWA_KV_HEADS * SWA_DIM
IN_WIDTH_R = _OFF_KR + LANES


def _nt_dot(a, b):
    return lax.dot_general(a, b, (((1,), (1,)), ((), ())), preferred_element_type=F32)


def _rms(c, g):
    r = lax.rsqrt(jnp.mean(c * c, axis=-1, keepdims=True) + RMS_EPS)
    return c * r * g


def _layer_norm(y, g, b):
    mu = jnp.mean(y, axis=-1, keepdims=True)
    d = y - mu
    var = jnp.mean(d * d, axis=-1, keepdims=True)
    return d * lax.rsqrt(var + LN_EPS) * g + b


def _rope_tables(pos_row, invf_col):
    tm = pos_row.shape[1]
    ang = invf_col * pos_row
    cos_t = jnp.cos(ang)
    sin_t = jnp.sin(ang)
    pad = jnp.zeros((LANES - NOPE - ROPE, tm), F32)
    c_tab = jnp.concatenate([jnp.ones((NOPE, tm), F32), cos_t, cos_t, pad], axis=0).T
    s_tab = jnp.concatenate([jnp.zeros((NOPE, tm), F32), -sin_t, sin_t, pad], axis=0).T
    return c_tab, s_tab


def _proj_kernel(x_ref, pos_ref, invf_ref, win_ref, gq_ref, wq_ref, gkv_ref, wkv_ref,
                 q_ref, k_ref, v_ref, qs_ref, ks_ref, vs_ref):
    tm = x_ref.shape[0]
    xb = x_ref[...].astype(BF16)
    proj = jnp.dot(xb, win_ref[...], preferred_element_type=F32)
    c_tab, s_tab = _rope_tables(pos_ref[0].astype(F32), invf_ref[...])
    lane = lax.broadcasted_iota(jnp.int32, (tm, LANES), 1)
    lo = lane < SWA_DIM

    cqn = _rms(proj[:, _OFF_CQ:_OFF_CQ + Q_LORA], gq_ref[...]).astype(BF16)
    q = jnp.dot(cqn, wq_ref[...], preferred_element_type=F32)
    q_scale = (NOPE + ROPE) ** -0.5 * LOG2_E
    for h in range(MLA_HEADS):
        qh = q[:, h * LANES:(h + 1) * LANES]
        qsw = pltpu.roll(qh, LANES - ROPE, 1)
        q_ref[:, h * LANES:(h + 1) * LANES] = ((qh * c_tab + qsw * s_tab) * q_scale).astype(BF16)

    ckvn = _rms(proj[:, _OFF_CKV:_OFF_CKV + KV_LORA], gkv_ref[...]).astype(BF16)
    kv = jnp.dot(ckvn, wkv_ref[...], preferred_element_type=F32)
    kr_blk = proj[:, _OFF_KR:_OFF_KR + LANES]
    kr_sw = pltpu.roll(kr_blk, LANES // 2, 1)
    rope_lanes = (lane >= NOPE) & (lane < NOPE + ROPE)
    kr = jnp.where(rope_lanes, kr_blk * c_tab + kr_sw * s_tab, 0.0)
    for h in range(MLA_HEADS):
        k_ref[:, h * LANES:(h + 1) * LANES] = (kv[:, h * LANES:(h + 1) * LANES] + kr).astype(BF16)
    v_ref[...] = kv[:, MLA_HEADS * LANES:].astype(BF16)

    qs_scale = SWA_DIM ** -0.5 * LOG2_E
    qs_ref[...] = (proj[:, _OFF_QS:_OFF_QS + SWA_HEADS * SWA_DIM] * qs_scale).astype(BF16)
    ks = proj[:, _OFF_KS:_OFF_KS + LANES]
    ksr = pltpu.roll(ks, LANES // 2, 1)
    ks_ref[:, 0 * LANES:1 * LANES] = jnp.where(lo, ks, 0.0).astype(BF16)
    ks_ref[:, 1 * LANES:2 * LANES] = jnp.where(lo, 0.0, ksr).astype(BF16)
    ks_ref[:, 2 * LANES:3 * LANES] = jnp.where(lo, ksr, 0.0).astype(BF16)
    ks_ref[:, 3 * LANES:4 * LANES] = jnp.where(lo, 0.0, ks).astype(BF16)
    vs = proj[:, _OFF_VS:_OFF_VS + LANES]
    vsr = pltpu.roll(vs, LANES // 2, 1)
    vs_ref[:, 0 * LANES:1 * LANES] = jnp.where(lo, vs, 1.0).astype(BF16)
    vs_ref[:, 1 * LANES:2 * LANES] = jnp.where(lo, vsr, 1.0).astype(BF16)


def _relayout_weights(w_in, w_q_b, w_kv_b):
    d = w_in.shape[0]
    o = np.cumsum((Q_LORA, KV_LORA, ROPE, SWA_HEADS * SWA_DIM, SWA_KV_HEADS * SWA_DIM,
                   SWA_KV_HEADS * SWA_DIM)).tolist()
    w_cq, w_ckv, w_kr = w_in[:, :o[0]], w_in[:, o[0]:o[1]], w_in[:, o[1]:o[2]]
    w_qs, w_ks, w_vs = w_in[:, o[2]:o[3]], w_in[:, o[3]:o[4]], w_in[:, o[4]:o[5]]
    x1, x2 = w_kr[:, :HALF_ROPE], w_kr[:, HALF_ROPE:]
    z32 = jnp.zeros((d, LANES // 2 - ROPE), w_in.dtype)
    kr_blk = jnp.concatenate([x2, x1, z32, x1, x2, z32], axis=1)
    win_r = jnp.concatenate([w_cq, w_ckv, w_qs, w_ks, w_vs, kr_blk], axis=1).astype(BF16)

    wq = w_q_b.reshape(Q_LORA, MLA_HEADS, NOPE + ROPE)
    qn, q1, q2 = wq[..., :NOPE], wq[..., NOPE:NOPE + HALF_ROPE], wq[..., NOPE + HALF_ROPE:]
    wq_r = jnp.concatenate([qn, q1, q2, q2, q1], axis=-1).reshape(Q_LORA, MLA_HEADS * LANES).astype(BF16)

    wkv = w_kv_b.reshape(KV_LORA, MLA_HEADS, NOPE + V_DIM)
    kn = jnp.concatenate([wkv[..., :NOPE], jnp.zeros((KV_LORA, MLA_HEADS, LANES - NOPE), w_kv_b.dtype)], axis=-1)
    wkv_r = jnp.concatenate([kn.reshape(KV_LORA, MLA_HEADS * LANES),
                             wkv[..., NOPE:].reshape(KV_LORA, MLA_HEADS * V_DIM)], axis=1).astype(BF16)
    return win_r, wq_r, wkv_r


def _projections(x2d, positions, win_r, gq, wq_r, gkv, wkv_r):
    n = x2d.shape[0]
    tm = PROJ_TM
    inv_freq = ROPE_THETA ** (-jnp.arange(0, ROPE, 2, dtype=F32) / ROPE)
    invf_col = inv_freq.reshape(HALF_ROPE, 1)
    pos_rows = positions.reshape(n // tm, 1, tm)
    row = lambda w: pl.BlockSpec((tm, w), lambda i: (i, 0))
    full = lambda a: pl.BlockSpec(a.shape, lambda i: (0,) * a.ndim, pipeline_mode=pl.Buffered(1))
    widths = (MLA_HEADS * LANES, MLA_HEADS * LANES, MLA_HEADS * V_DIM,
              SWA_HEADS * SWA_DIM, 4 * LANES, 2 * LANES)
    return pl.pallas_call(
        _proj_kernel,
        grid=(n // tm,),
        in_specs=[row(D_MODEL), pl.BlockSpec((1, 1, tm), lambda i: (i, 0, 0)), full(invf_col),
                  full(win_r), full(gq), full(wq_r), full(gkv), full(wkv_r)],
        out_specs=[row(w) for w in widths],
        out_shape=[jax.ShapeDtypeStruct((n, w), BF16) for w in widths],
        compiler_params=pltpu.CompilerParams(dimension_semantics=("parallel",),
                                             vmem_limit_bytes=VMEM_LIMIT),
        name="proj",
    )(x2d, pos_rows, invf_col, win_r, gq, wq_r, gkv, wkv_r)


def _mla_body(q_ref, k_ref, v_ref, o_ref, vt_ref, st_ref, acc_ref):
    tq = q_ref.shape[1]
    tk = MLA_TK
    nc = k_ref.shape[1] // tk

    @pl.when(pl.program_id(2) == 0)
    def _():
        ones = jnp.ones((MLA_VT_ROWS - V_DIM, tk), BF16)
        for c in range(nc):
            vt = v_ref[0, c * tk:(c + 1) * tk, :].astype(F32).T.astype(BF16)
            for hh in range(2):
                vt_ref[c, hh, 0:V_DIM, :] = vt[hh * V_DIM:(hh + 1) * V_DIM]
                vt_ref[c, hh, V_DIM:MLA_VT_ROWS, :] = ones

    nbuf = st_ref.shape[0]
    ahead = nbuf - 1
    outs = []
    for hh in range(2):
        q = q_ref[0, :, hh * LANES:(hh + 1) * LANES]

        def scores(c):
            st_ref[c % nbuf] = _nt_dot(k_ref[0, c * tk:(c + 1) * tk, hh * LANES:(hh + 1) * LANES], q)

        def consume(c, m):
            st = st_ref[c % nbuf]
            m_new = jnp.maximum(m, jnp.max(st, axis=0, keepdims=True))
            p = jnp.exp2(st - m_new).astype(BF16)
            pv = jnp.dot(vt_ref[c, hh], p, preferred_element_type=F32)
            acc_ref[...] = jnp.exp2(m - m_new) * acc_ref[...] + pv
            return m_new

        acc_ref[...] = jnp.zeros_like(acc_ref)
        for c in range(ahead):
            scores(c)
        m = jnp.full((1, tq), -jnp.inf, F32)
        for c in range(nc):
            if c + ahead < nc:
                scores(c + ahead)
            m = consume(c, m)
        outs.append(acc_ref[0:V_DIM, :] / acc_ref[V_DIM:V_DIM + 1, :])
    o_ref[0] = jnp.concatenate(outs, axis=0).T.astype(BF16)


def _mla_attention(q, k, v):
    b, s, _ = q.shape
    tq = MLA_TQ
    return pl.pallas_call(
        _mla_body,
        grid=(b, MLA_HEADS // 2, s // tq),
        in_specs=[pl.BlockSpec((1, tq, 2 * LANES), lambda bi, hp, i: (bi, i, hp)),
                  pl.BlockSpec((1, s, 2 * LANES), lambda bi, hp, i: (bi, 0, hp)),
                  pl.BlockSpec((1, s, LANES), lambda bi, hp, i: (bi, 0, hp))],
        out_specs=pl.BlockSpec((1, tq, LANES), lambda bi, hp, i: (bi, i, hp)),
        out_shape=jax.ShapeDtypeStruct((b, s, MLA_HEADS * V_DIM), BF16),
        scratch_shapes=[pltpu.VMEM((s // MLA_TK, 2, MLA_VT_ROWS, MLA_TK), BF16),
                        pltpu.VMEM((MLA_SCORE_BUFS, MLA_TK, tq), F32),
                        pltpu.VMEM((MLA_VT_ROWS, tq), F32)],
        compiler_params=pltpu.CompilerParams(
            dimension_semantics=("parallel", "parallel", "arbitrary"),
            vmem_limit_bytes=VMEM_LIMIT),
        name="mla_attn",
    )(q, k, v)


def _alibi_slopes(n_heads):
    return 2.0 ** (-8.0 * (np.arange(n_heads, dtype=np.float32) + 1.0) / n_heads)


def _swa_body(first_of_batch, blocks, sink_ref, qs_ref, ks_ref, vs_ref, pcol_ref, prow_ref, o_ref,
              vt_ref, s_ref):
    nb = prow_ref.shape[1]
    win = 3 * BAND
    ki = lax.broadcasted_iota(jnp.int32, (win, BAND), 0)
    qi = lax.broadcasted_iota(jnp.int32, (win, BAND), 1)
    slopes = _alibi_slopes(SWA_HEADS)

    @pl.when(first_of_batch)
    def _():
        for kvh in range(SWA_KV_HEADS):
            for blk in range(nb):
                vt = vs_ref[0, blk * BAND:(blk + 1) * BAND, kvh * LANES:(kvh + 1) * LANES]
                vt_ref[kvh, blk] = vt.astype(F32).T[0:SWA_VT_ROWS].astype(BF16)

    units = [(i, r0, kvh) for i, r0 in blocks for kvh in range(SWA_KV_HEADS)]
    nbuf = s_ref.shape[0]
    ahead = nbuf - 1

    def window(i):
        start_blk = jnp.clip(i - 1, 0, nb - 3)
        return start_blk, pl.multiple_of(start_blk * BAND, BAND)

    def scores(u):
        i, r0, kvh = units[u]
        _, start = window(i)
        q_pairs = jnp.concatenate(
            [qs_ref[0, r0:r0 + BAND, (2 * kvh) * LANES:(2 * kvh + 1) * LANES],
             qs_ref[0, r0:r0 + BAND, (2 * kvh + 1) * LANES:(2 * kvh + 2) * LANES]], axis=0)
        k_lo = ks_ref[0, pl.ds(start, win), (2 * kvh) * LANES:(2 * kvh + 1) * LANES]
        k_hi = ks_ref[0, pl.ds(start, win), (2 * kvh + 1) * LANES:(2 * kvh + 2) * LANES]
        s_ref[u % nbuf, 0] = _nt_dot(k_lo, q_pairs)
        s_ref[u % nbuf, 1] = _nt_dot(k_hi, q_pairs)

    def consume(u):
        i, r0, kvh = units[u]
        start_blk, start = window(i)
        pk = pcol_ref[0, pl.ds(start, win), :].astype(F32)
        pq = prow_ref[0, pl.ds(i, 1), :].astype(F32)
        rel = ki - qi + (start - i * BAND)
        dist = jnp.where(jnp.abs(rel) <= WINDOW, jnp.abs(pk - pq), MASK_DIST)
        es, sink_w = [], []
        for g in range(SWA_GROUP):
            h = kvh * SWA_GROUP + g
            pair, par = g // 2, g % 2
            s = s_ref[u % nbuf, par, :, pair * BAND:(pair + 1) * BAND]
            s = s - float(slopes[h] * LOG2_E) * dist
            sk = sink_ref[h] * LOG2_E
            m = jnp.maximum(jnp.max(s, axis=0, keepdims=True), sk)
            sink_w.append(jnp.exp2(sk - m))
            es.append(jnp.exp2(s - m).astype(BF16))
        vt_win = jnp.concatenate([vt_ref[kvh, start_blk + j] for j in range(3)], axis=1)
        ot = jnp.dot(vt_win, jnp.concatenate(es, axis=1), preferred_element_type=F32)
        for pair in range(2):
            heads = []
            for par in range(2):
                g = 2 * pair + par
                og = ot[:, g * BAND:(g + 1) * BAND]
                heads.append(og[0:SWA_DIM] / (og[SWA_DIM:SWA_DIM + 1] + sink_w[g]))
            col = (kvh * 2 + pair) * LANES
            o_ref[0, r0:r0 + BAND, col:col + LANES] = jnp.concatenate(heads, axis=0).T.astype(BF16)

    for u in range(min(ahead, len(units))):
        scores(u)
    for u in range(len(units)):
        if u + ahead < len(units):
            scores(u + ahead)
        consume(u)


def _swa_kernel(sink_ref, qs_ref, ks_ref, vs_ref, pcol_ref, prow_ref, o_ref, vt_ref, s_ref):
    sub = qs_ref.shape[1] // BAND
    step = pl.program_id(1)
    _swa_body(step == 0, [(step * sub + jb, jb * BAND) for jb in range(sub)],
              sink_ref, qs_ref, ks_ref, vs_ref, pcol_ref, prow_ref, o_ref, vt_ref, s_ref)


def _swa_attention(qs, ks4, vs2, positions, sinks):
    b, s, _ = qs.shape
    nb = s // BAND
    tq = SWA_TQ
    pcol = positions.reshape(b, s, 1)
    prow = positions.reshape(b, nb, BAND)
    return pl.pallas_call(
        _swa_kernel,
        grid=(b, s // tq),
        in_specs=[pl.BlockSpec(memory_space=pltpu.SMEM),
                  pl.BlockSpec((1, tq, SWA_HEADS * SWA_DIM), lambda bi, i: (bi, i, 0)),
                  pl.BlockSpec((1, s, 4 * LANES), lambda bi, i: (bi, 0, 0)),
                  pl.BlockSpec((1, s, 2 * LANES), lambda bi, i: (bi, 0, 0)),
                  pl.BlockSpec((1, s, 1), lambda bi, i: (bi, 0, 0)),
                  pl.BlockSpec((1, nb, BAND), lambda bi, i: (bi, 0, 0))],
        out_specs=pl.BlockSpec((1, tq, SWA_HEADS * SWA_DIM), lambda bi, i: (bi, i, 0)),
        out_shape=jax.ShapeDtypeStruct((b, s, SWA_HEADS * SWA_DIM), BF16),
        scratch_shapes=[pltpu.VMEM((SWA_KV_HEADS, nb, SWA_VT_ROWS, BAND), BF16),
                        pltpu.VMEM((SWA_SCORE_BUFS, 2, 3 * BAND, 2 * LANES), F32)],
        compiler_params=pltpu.CompilerParams(dimension_semantics=("parallel", "arbitrary"),
                                             vmem_limit_bytes=VMEM_LIMIT),
        name="swa_attn",
    )(sinks.astype(F32), qs, ks4, vs2, pcol, prow)


def _mix_ffn_kernel(seq_tiles, x_ref, xp_ref, xn_ref, om_ref, omp_ref, omn_ref, os_ref, osp_ref,
                    osn_ref, wo1_ref, wo2_ref, g1_ref, b1_ref, wup_ref, cw_ref, cb_ref, wdn_ref,
                    g2_ref, b2_ref, y_ref, h_ref, ai_ref, a_ref):
    tm = x_ref.shape[0]
    fc = FFN_FC
    half = tm // 2
    nslab = fc // LANES
    rows = tm + 2 * HALO
    i = pl.program_id(0)
    t = i % seq_tiles

    nblk = tm // LN_ROWS

    def ext(r, main_ref, prev_ref, next_ref):
        parts = [main_ref[r * LN_ROWS:(r + 1) * LN_ROWS, :]]
        if r == 0:
            parts.insert(0, prev_ref[...])
        if r == nblk - 1:
            parts.append(next_ref[...])
        return parts[0] if len(parts) == 1 else jnp.concatenate(parts, axis=0)

    x1e_blocks = []
    for r in range(nblk):
        mix = jnp.dot(ext(r, om_ref, omp_ref, omn_ref), wo1_ref[...], preferred_element_type=F32)
        mix = mix + jnp.dot(ext(r, os_ref, osp_ref, osn_ref), wo2_ref[...], preferred_element_type=F32)
        blk = _layer_norm(ALPHA * ext(r, x_ref, xp_ref, xn_ref) + mix, g1_ref[...], b1_ref[...])
        row = lax.broadcasted_iota(jnp.int32, (blk.shape[0], 1), 0)
        if r == 0:
            blk = jnp.where((t == 0) & (row < HALO), 0.0, blk)
        if r == nblk - 1:
            first_next = blk.shape[0] - HALO
            blk = jnp.where((t == seq_tiles - 1) & (row >= first_next), 0.0, blk)
        x1e_blocks.append(blk)
    xe_blocks = [blk.astype(BF16) for blk in x1e_blocks]
    xe = jnp.concatenate(xe_blocks, axis=0)
    x1_blocks = list(x1e_blocks)
    x1_blocks[0] = x1_blocks[0][HALO:]
    x1_blocks[-1] = x1_blocks[-1][:x1_blocks[-1].shape[0] - HALO]

    def up_proj(j, cols):
        w = wup_ref[:, cols:cols + fc]
        if j == 0:
            return jnp.concatenate([jnp.dot(xb, w, preferred_element_type=F32) for xb in xe_blocks], axis=0)
        return jnp.dot(xe, w, preferred_element_type=F32)

    def conv_even_odd(slot, slab, col):
        e = [h_ref[slot, slab, pl.ds(HALO - 1 + k, half, stride=2), :] for k in range(4)]
        w = cw_ref[:, col:col + LANES]
        bias = cb_ref[:, col:col + LANES]
        even = e[0] * w[0:1] + e[1] * w[1:2] + e[2] * w[2:3] + bias
        odd = e[1] * w[0:1] + e[2] * w[1:2] + e[3] * w[2:3] + bias
        return even, odd

    def gate(gt, ut):
        return 0.5 * gt * (1.0 + lax.erf(gt * (2.0 ** -0.5))) * ut

    for j in range(FFN_NCHUNK):
        slot = j % 2
        cg, cu = j * fc, D_FF + j * fc
        hg = up_proj(j, cg)
        hu = up_proj(j, cu)
        for sl in range(nslab):
            h_ref[slot, sl] = hg[:, sl * LANES:(sl + 1) * LANES]
            h_ref[slot, nslab + sl] = hu[:, sl * LANES:(sl + 1) * LANES]
        for sl in range(nslab):
            g_even, g_odd = conv_even_odd(slot, sl, cg + sl * LANES)
            u_even, u_odd = conv_even_odd(slot, nslab + sl, cu + sl * LANES)
            ai_ref[slot, sl, pl.ds(0, half, stride=2), :] = gate(g_even, u_even)
            ai_ref[slot, sl, pl.ds(1, half, stride=2), :] = gate(g_odd, u_odd)
            a_ref[:, cg + sl * LANES:cg + (sl + 1) * LANES] = ai_ref[slot, sl].astype(BF16)

    for r in range(nblk):
        blk_rows = slice(r * LN_ROWS, (r + 1) * LN_ROWS)
        ff = jnp.dot(a_ref[blk_rows, :], wdn_ref[...], preferred_element_type=F32)
        y_ref[blk_rows, :] = _layer_norm(ALPHA * x1_blocks[r] + ff, g2_ref[...], b2_ref[...])


def _mix_ffn(x2d, om, osw, seq_len, wo1, wo2, g1, b1, wup, cw, cb, wdn, g2, b2):
    n = x2d.shape[0]
    tm = FFN_TM
    seq_tiles = seq_len // tm
    hb = tm // HALO
    nhb = n // HALO
    full = lambda a: pl.BlockSpec(a.shape, lambda i: (0,) * a.ndim, pipeline_mode=pl.Buffered(1))

    def tile_and_halos(width):
        return [pl.BlockSpec((tm, width), lambda i: (i, 0)),
                pl.BlockSpec((HALO, width), lambda i: (jnp.maximum(i * hb - 1, 0), 0)),
                pl.BlockSpec((HALO, width), lambda i: (jnp.minimum((i + 1) * hb, nhb - 1), 0))]

    weights = (wo1, wo2, g1, b1, wup, cw, cb, wdn, g2, b2)
    return pl.pallas_call(
        functools.partial(_mix_ffn_kernel, seq_tiles),
        grid=(n // tm,),
        in_specs=(tile_and_halos(D_MODEL) + tile_and_halos(om.shape[1]) + tile_and_halos(osw.shape[1])
                  + [full(w) for w in weights]),
        out_specs=pl.BlockSpec((tm, D_MODEL), lambda i: (i, 0)),
        out_shape=jax.ShapeDtypeStruct((n, D_MODEL), F32),
        scratch_shapes=[pltpu.VMEM((2, 2 * FFN_FC // LANES, tm + 2 * HALO, LANES), F32),
                        pltpu.VMEM((2, FFN_FC // LANES, tm, LANES), F32),
                        pltpu.VMEM((tm, D_FF), BF16)],
        compiler_params=pltpu.CompilerParams(dimension_semantics=("parallel",),
                                             vmem_limit_bytes=MIX_FFN_VMEM_LIMIT),
        name="mix_ffn",
    )(x2d, x2d, x2d, om, om, om, osw, osw, osw, *weights)


def kernel(x, positions, w_in, q_norm_g, w_q_b, kv_norm_g, w_kv_b, swa_sinks, w_o,
           ln1_g, ln1_b, w_up, conv_w, conv_b, w_down, ln2_g, ln2_b):
    b, s, d = x.shape
    n = b * s
    x2d = x.reshape(n, d)

    win_r, wq_r, wkv_r = _relayout_weights(w_in, w_q_b, w_kv_b)
    q, k, v, qs, ks4, vs2 = _projections(
        x2d, positions, win_r, q_norm_g.reshape(1, -1), wq_r, kv_norm_g.reshape(1, -1), wkv_r)

    o_mla = _mla_attention(q.reshape(b, s, -1), k.reshape(b, s, -1), v.reshape(b, s, -1))
    o_swa = _swa_attention(qs.reshape(b, s, -1), ks4.reshape(b, s, -1), vs2.reshape(b, s, -1),
                           positions, swa_sinks)

    n_mla = MLA_HEADS * V_DIM
    wo = w_o.astype(BF16)
    y = _mix_ffn(x2d, o_mla.reshape(n, -1), o_swa.reshape(n, -1), s, wo[:n_mla], wo[n_mla:],
                 ln1_g.reshape(1, -1), ln1_b.reshape(1, -1),
                 w_up.astype(BF16), conv_w.reshape(3, 2 * D_FF), conv_b.reshape(1, 2 * D_FF),
                 w_down.astype(BF16), ln2_g.reshape(1, -1), ln2_b.reshape(1, -1))
    return y.reshape(b, s, d)
```
